```python
import math
import jax, jax.numpy as jnp
from jax import lax
import numpy as np

D_MODEL = 1024
BATCH = 2
SEQ = 16384
DEPTH = 4

N_EVEN = (DEPTH + 1) // 2
N_ODD = DEPTH // 2
MIX_WIDTH = D_MODEL
HALF = MIX_WIDTH // 2

ML_HEADS = 4
ML_HD = HALF // ML_HEADS
ML_CHUNK = 64
FOX_HEADS = 8
FOX_HD = HALF // FOX_HEADS
FOX_QBLOCK = 128
SSM_HEADDIM = 64
SSM_HEADS = HALF // SSM_HEADDIM
SSM_GROUPS = 2
SSM_HPG = SSM_HEADS // SSM_GROUPS
SSM_STATE = 128
SSM_CONV = 4
SSM_CHUNK = 128
SSM_CONV_CH = HALF + 2 * SSM_GROUPS * SSM_STATE
POOL_WINDOWS = (2, 4, 8, 16)
POOL_GROUPS = 4
POOL_GW = HALF // POOL_GROUPS
D_FF = 256 * ((8 * D_MODEL // 3 + 255) // 256)
FFN_CONV = 3

EVEN_IN = 4 * HALF + 2 * ML_HEADS + 3 * HALF + FOX_HEADS
ODD_IN = HALF + SSM_CONV_CH + SSM_HEADS + HALF
LN_EPS = 1e-5
ALPHA = (2.0 * DEPTH) ** 0.25
BETA = (8.0 * DEPTH) ** -0.25

kernel_name = 'hybrid_mlstm_fox_mamba2_pool_deepnorm'


def split_sizes(x, sizes):
    offs = []
    acc = 0
    for s in sizes[:-1]:
        acc += s
        offs.append(acc)
    return jnp.split(x, offs, axis=-1)


def layer_norm(x, g, b):
    xf = x.astype(jnp.float32)
    mu = xf.mean(-1, keepdims=True)
    var = jnp.square(xf - mu).mean(-1, keepdims=True)
    return ((xf - mu) * lax.rsqrt(var + LN_EPS) * g.astype(jnp.float32) + b.astype(jnp.float32)).astype(x.dtype)


def rms_norm(x, g):
    xf = x.astype(jnp.float32)
    return xf * lax.rsqrt(jnp.square(xf).mean(-1, keepdims=True) + LN_EPS) * g.astype(jnp.float32)


def causal_dwconv(x, w, b):
    k, c = w.shape
    y = lax.conv_general_dilated(x, w[:, None, :].astype(x.dtype), window_strides=(1,), padding=[(k - 1, 0)], dimension_numbers=('NWC', 'WIO', 'NWC'), feature_group_count=c)
    return y + b.astype(x.dtype)


def mlstm_chunkwise(q, k, v, i_pre, f_pre):
    b, h, s, d = q.shape
    L = ML_CHUNK
    nc = s // L
    q = q.astype(jnp.float32) * (d ** -0.5)
    k = k.astype(jnp.float32)
    v = v.astype(jnp.float32)
    ig = i_pre.astype(jnp.float32)
    logf = jax.nn.log_sigmoid(f_pre.astype(jnp.float32))

    def to_chunks(t):
        return jnp.moveaxis(t.reshape(b, h, nc, L, *t.shape[3:]), 2, 0)

    causal = jnp.tril(jnp.ones((L, L), dtype=bool))

    def step(carry, inp):
        c_st, n_st, m_st = carry
        qq, kk, vv, ii, ff = inp
        bcum = jnp.cumsum(ff, axis=-1)
        a = bcum + m_st[..., None]
        dmat = jnp.where(causal, bcum[..., :, None] - bcum[..., None, :] + ii[..., None, :], -jnp.inf)
        mt = jnp.maximum(a, dmat.max(-1))
        w_intra = jnp.exp(dmat - mt[..., None])
        w_inter = jnp.exp(a - mt)
        qk = jnp.einsum('bhld,bhsd->bhls', qq, kk) * w_intra
        num = jnp.einsum('bhls,bhsd->bhld', qk, vv) + w_inter[..., None] * jnp.einsum('bhvk,bhlk->bhlv', c_st, qq)
        den = qk.sum(-1) + w_inter * jnp.einsum('bhk,bhlk->bhl', n_st, qq)
        h_out = num / jnp.maximum(jnp.abs(den), jnp.exp(-mt))[..., None]
        btot = bcum[..., -1]
        g_log = btot[..., None] - bcum + ii
        m_new = jnp.maximum(btot + m_st, g_log.max(-1))
        ws = jnp.exp(g_log - m_new[..., None])
        decay = jnp.exp(btot + m_st - m_new)
        c_new = decay[..., None, None] * c_st + jnp.einsum('bhl,bhlv,bhlk->bhvk', ws, vv, kk)
        n_new = decay[..., None] * n_st + jnp.einsum('bhl,bhlk->bhk', ws, kk)
        return (c_new, n_new, m_new), h_out

    init = (jnp.zeros((b, h, d, d), jnp.float32), jnp.zeros((b, h, d), jnp.float32), jnp.zeros((b, h), jnp.float32))
    _, hs = lax.scan(step, init, (to_chunks(q), to_chunks(k), to_chunks(v), to_chunks(ig), to_chunks(logf)))
    return jnp.moveaxis(hs, 0, 2).reshape(b, h, s, d)


def forgetting_attention(q, k, v, f_pre):
    b, h, s, d = q.shape
    F = jnp.cumsum(jax.nn.log_sigmoid(f_pre.astype(jnp.float32)), axis=-1)
    nb = s // FOX_QBLOCK
    kpos = jnp.arange(s)
    scale = d ** -0.5

    def block(i):
        start = i * FOX_QBLOCK
        qb = lax.dynamic_slice_in_dim(q, start, FOX_QBLOCK, axis=2)
        fq = lax.dynamic_slice_in_dim(F, start, FOX_QBLOCK, axis=2)
        qpos = start + jnp.arange(FOX_QBLOCK)
        logits = jnp.einsum('bhqd,bhkd->bhqk', qb, k).astype(jnp.float32) * scale + fq[..., :, None] - F[..., None, :]
        logits = jnp.where(kpos[None, :] <= qpos[:, None], logits, -jnp.inf)
        p = jax.nn.softmax(logits, axis=-1)
        return jnp.einsum('bhqk,bhkd->bhqd', p.astype(v.dtype), v)

    out = lax.map(block, jnp.arange(nb))
    return jnp.moveaxis(out, 0, 2).reshape(b, h, s, d)


def segsum(a):
    L = a.shape[-1]
    cs = jnp.cumsum(a, axis=-1)
    mask = jnp.tril(jnp.ones((L, L), dtype=bool))
    return jnp.where(mask, cs[..., :, None] - cs[..., None, :], -jnp.inf)


def ssd_chunked(x, dt_a, bm, cm):
    b, s, g, e, p = x.shape
    L = SSM_CHUNK
    nc = s // L
    x = x.reshape(b, nc, L, g, e, p)
    bm = bm.reshape(b, nc, L, g, -1)
    cm = cm.reshape(b, nc, L, g, -1)
    a = jnp.moveaxis(dt_a.reshape(b, nc, L, g, e), (1, 2), (3, 4))
    a_cs = jnp.cumsum(a, axis=-1)
    lmat = jnp.exp(segsum(a))
    cb = jnp.einsum('bclgn,bcsgn->bgcls', cm, bm)
    y_diag = jnp.einsum('bgcls,bgecls,bcsgep->bclgep', cb, lmat, x)
    decay_states = jnp.exp(a_cs[..., -1:] - a_cs)
    states = jnp.einsum('bclgn,bgecl,bclgep->bcgepn', bm, decay_states, x)
    chunk_decay = jnp.exp(a_cs[..., -1])

    def step(hst, inp):
        st, dec = inp
        return dec[..., None, None] * hst + st, hst

    h0 = jnp.zeros((b, g, e, p, states.shape[-1]), jnp.float32)
    _, prev = lax.scan(step, h0, (jnp.moveaxis(states, 1, 0), jnp.moveaxis(chunk_decay, 3, 0)))
    y_off = jnp.einsum('bclgn,cbgepn,bgecl->bclgep', cm, prev, jnp.exp(a_cs))
    return (y_diag + y_off).reshape(b, s, g, e, p)


def multiscale_pool(u, w_grp, b_grp, scale):
    b, s, c = u.shape
    uf = u.astype(jnp.float32)
    csz = jnp.concatenate([jnp.zeros((b, 1, c), jnp.float32), jnp.cumsum(uf, axis=1)], axis=1)
    t = jnp.arange(s)
    outs = []
    for gi, w in enumerate(POOL_WINDOWS):
        lo_c, hi_c = gi * POOL_GW, (gi + 1) * POOL_GW
        cg = csz[:, :, lo_c:hi_c]
        lo = jnp.maximum(t + 1 - w, 0)
        wsum = cg[:, 1:] - jnp.take(cg, lo, axis=1)
        cnt = jnp.minimum(t + 1, w).astype(jnp.float32)
        outs.append(wsum / cnt[None, :, None] - uf[:, :, lo_c:hi_c])
    pooled = jnp.stack(outs, axis=2)
    y = jnp.einsum('bsgc,gcd->bsgd', pooled, w_grp.astype(jnp.float32)).reshape(b, s, c) + b_grp.astype(jnp.float32)
    return y * scale.astype(jnp.float32)


def head_norm(h, g):
    nh, d = h.shape[1], h.shape[3]
    mu = h.mean(-1, keepdims=True)
    var = jnp.square(h - mu).mean(-1, keepdims=True)
    return (h - mu) * lax.rsqrt(var + LN_EPS) * g.astype(jnp.float32).reshape(nh, 1, d)


def even_mixer(x, w_in, b_in, ml_norm, w_out):
    b, s, _ = x.shape
    u = x @ w_in + b_in
    mq, mk, mv, mo, mi, mf, fq, fk, fv, ff = split_sizes(u, [HALF, HALF, HALF, HALF, ML_HEADS, ML_HEADS, HALF, HALF, HALF, FOX_HEADS])

    def heads(t, nh):
        return t.reshape(b, s, nh, -1).transpose(0, 2, 1, 3)

    h_ml = mlstm_chunkwise(heads(mq, ML_HEADS), heads(mk, ML_HEADS), heads(mv, ML_HEADS), mi.transpose(0, 2, 1), mf.transpose(0, 2, 1))
    h_ml = head_norm(h_ml, ml_norm).transpose(0, 2, 1, 3).reshape(b, s, HALF)
    h_ml = (jax.nn.sigmoid(mo.astype(jnp.float32)) * h_ml).astype(x.dtype)
    h_fx = forgetting_attention(heads(fq, FOX_HEADS), heads(fk, FOX_HEADS), heads(fv, FOX_HEADS), ff.transpose(0, 2, 1))
    h_fx = h_fx.transpose(0, 2, 1, 3).reshape(b, s, HALF).astype(x.dtype)
    return jnp.concatenate([h_ml, h_fx], axis=-1) @ w_out


def odd_mixer(x, w_in, conv_w, conv_b, dt_bias, a_log, d_skip, ssm_norm, pool_w, pool_b, pool_scale, w_out):
    b, s, _ = x.shape
    u = x @ w_in
    z, xbc, dt_raw, pool_in = split_sizes(u, [HALF, SSM_CONV_CH, SSM_HEADS, HALF])
    xbc = jax.nn.silu(causal_dwconv(xbc, conv_w, conv_b)).astype(jnp.float32)
    xs, bm, cm = split_sizes(xbc, [HALF, SSM_GROUPS * SSM_STATE, SSM_GROUPS * SSM_STATE])
    dt = jax.nn.softplus(dt_raw.astype(jnp.float32) + dt_bias.astype(jnp.float32))
    a = -jnp.exp(a_log.astype(jnp.float32)).reshape(SSM_GROUPS, SSM_HPG)
    xh = xs.reshape(b, s, SSM_GROUPS, SSM_HPG, SSM_HEADDIM)
    dth = dt.reshape(b, s, SSM_GROUPS, SSM_HPG)
    y = ssd_chunked(xh * dth[..., None], dth * a, bm.reshape(b, s, SSM_GROUPS, SSM_STATE), cm.reshape(b, s, SSM_GROUPS, SSM_STATE))
    y = y + d_skip.astype(jnp.float32).reshape(SSM_GROUPS, SSM_HPG, 1) * xh
    y = rms_norm(y.reshape(b, s, HALF) * jax.nn.silu(z.astype(jnp.float32)), ssm_norm).astype(x.dtype)
    p = multiscale_pool(pool_in, pool_w, pool_b, pool_scale).astype(x.dtype)
    return jnp.concatenate([y, p], axis=-1) @ w_out


def conv_ffn(x, w_up, conv_w, conv_b, w_down):
    h = causal_dwconv(x @ w_up, conv_w, conv_b)
    val, gate = jnp.split(h, 2, axis=-1)
    return (jax.nn.silu(gate) * val) @ w_down


def setup_inputs(seed: int = 0) -> dict:
    key = jax.random.key(seed)
    ks = jax.random.split(key, 40)
    E, O, L = N_EVEN, N_ODD, DEPTH
    nrm = lambda k, shape, sc: jax.random.normal(k, shape, jnp.float32) * sc
    x = nrm(ks[0], (BATCH, SEQ, D_MODEL), 1.0)
    ev_w_in = nrm(ks[1], (E, D_MODEL, EVEN_IN), D_MODEL ** -0.5)
    ev_b_in = jnp.concatenate([
        nrm(ks[2], (E, 4 * HALF), 0.02),
        nrm(ks[3], (E, ML_HEADS), 0.1),
        3.0 + 3.0 * jax.random.uniform(ks[4], (E, ML_HEADS), jnp.float32),
        nrm(ks[5], (E, 3 * HALF), 0.02),
        1.0 + 3.0 * jax.random.uniform(ks[6], (E, FOX_HEADS), jnp.float32)], axis=-1)
    ev_ml_norm = 1.0 + nrm(ks[7], (E, HALF), 0.02)
    ev_w_out = nrm(ks[8], (E, MIX_WIDTH, D_MODEL), MIX_WIDTH ** -0.5 * BETA)
    od_w_in = nrm(ks[9], (O, D_MODEL, ODD_IN), D_MODEL ** -0.5)
    od_conv_w = nrm(ks[10], (O, SSM_CONV, SSM_CONV_CH), SSM_CONV ** -0.5)
    od_conv_b = nrm(ks[11], (O, SSM_CONV_CH), 0.02)
    dt0 = jnp.exp(jax.random.uniform(ks[12], (O, SSM_HEADS), jnp.float32, math.log(1e-3), math.log(1e-1)))
    od_dt_bias = dt0 + jnp.log(-jnp.expm1(-dt0))
    od_a_log = jnp.log(jax.random.uniform(ks[13], (O, SSM_HEADS), jnp.float32, 1.0, 16.0))
    od_d_skip = 1.0 + nrm(ks[14], (O, SSM_HEADS), 0.02)
    od_ssm_norm = 1.0 + nrm(ks[15], (O, HALF), 0.02)
    od_pool_w = nrm(ks[16], (O, POOL_GROUPS, POOL_GW, POOL_GW), POOL_GW ** -0.5)
    od_pool_b = nrm(ks[17], (O, HALF), 0.02)
    od_pool_scale = 1.0 + nrm(ks[18], (O, HALF), 0.1)
    od_w_out = nrm(ks[19], (O, MIX_WIDTH, D_MODEL), MIX_WIDTH ** -0.5 * BETA)
    ffn_w_up = nrm(ks[20], (L, D_MODEL, 2 * D_FF), D_MODEL ** -0.5)
    ffn_conv_w = nrm(ks[21], (L, FFN_CONV, 2 * D_FF), FFN_CONV ** -0.5)
    ffn_conv_b = nrm(ks[22], (L, 2 * D_FF), 0.02)
    ffn_w_down = nrm(ks[23], (L, D_FF, D_MODEL), D_FF ** -0.5 * BETA)
    ln1_g = 1.0 + nrm(ks[24], (L, D_MODEL), 0.02)
    ln1_b = nrm(ks[25], (L, D_MODEL), 0.02)
    ln2_g = 1.0 + nrm(ks[26], (L, D_MODEL), 0.02)
    ln2_b = nrm(ks[27], (L, D_MODEL), 0.02)
    return {'x': x, 'ev_w_in': ev_w_in, 'ev_b_in': ev_b_in, 'ev_ml_norm': ev_ml_norm, 'ev_w_out': ev_w_out,
            'od_w_in': od_w_in, 'od_conv_w': od_conv_w, 'od_conv_b': od_conv_b, 'od_dt_bias': od_dt_bias,
            'od_a_log': od_a_log, 'od_d_skip': od_d_skip, 'od_ssm_norm': od_ssm_norm, 'od_pool_w': od_pool_w,
            'od_pool_b': od_pool_b, 'od_pool_scale': od_pool_scale, 'od_w_out': od_w_out,
            'ffn_w_up': ffn_w_up, 'ffn_conv_w': ffn_conv_w, 'ffn_conv_b': ffn_conv_b, 'ffn_w_down': ffn_w_down,
            'ln1_g': ln1_g, 'ln1_b': ln1_b, 'ln2_g': ln2_g, 'ln2_b': ln2_b}


def reference(x, ev_w_in, ev_b_in, ev_ml_norm, ev_w_out, od_w_in, od_conv_w, od_conv_b, od_dt_bias, od_a_log, od_d_skip, od_ssm_norm, od_pool_w, od_pool_b, od_pool_scale, od_w_out, ffn_w_up, ffn_conv_w, ffn_conv_b, ffn_w_down, ln1_g, ln1_b, ln2_g, ln2_b):
    for layer in range(DEPTH):
        j = layer // 2
        if layer % 2 == 0:
            y = even_mixer(x, ev_w_in[j], ev_b_in[j], ev_ml_norm[j], ev_w_out[j])
        else:
            y = odd_mixer(x, od_w_in[j], od_conv_w[j], od_conv_b[j], od_dt_bias[j], od_a_log[j], od_d_skip[j], od_ssm_norm[j], od_pool_w[j], od_pool_b[j], od_pool_scale[j], od_w_out[j])
        x = layer_norm(ALPHA * x + y, ln1_g[layer], ln1_b[layer])
        f = conv_ffn(x, ffn_w_up[layer], ffn_conv_w[layer], ffn_conv_b[layer], ffn_w_down[layer])
        x = layer_norm(ALPHA * x + f, ln2_g[layer], ln2_b[layer])
    return x
```

```python
import functools
import math

import jax
import jax.numpy as jnp
from jax import lax
from jax.experimental import pallas as pl
from jax.experimental.pallas import tpu as pltpu

F32 = jnp.float32
BF16 = jnp.bfloat16

D_MODEL = 1024
DEPTH = 4
HALF = D_MODEL // 2
ML_HEADS = 4
ML_HD = HALF // ML_HEADS
FOX_HEADS = 8
FOX_HD = HALF // FOX_HEADS
SSM_HEADDIM = 64
SSM_HEADS = HALF // SSM_HEADDIM
SSM_GROUPS = 2
SSM_STATE = 128
SSM_CONV = 4
SSM_CONV_CH = HALF + 2 * SSM_GROUPS * SSM_STATE
POOL_WINDOWS = (2, 4, 8, 16)
POOL_GW = HALF // len(POOL_WINDOWS)
D_FF = 256 * ((8 * D_MODEL // 3 + 255) // 256)
FFN_CONV = 3
LN_EPS = 1e-5
ALPHA = (2.0 * DEPTH) ** 0.25

LANES = 128
SUBLANES = 8
NEG = -1e30
LOG2E = 1.4426950408889634
VMEM_LIMIT = 56 * 1024 * 1024

ROW_TILE = 512
ML_CHUNK = 256
SSD_CHUNK = 256
FOX_BLOCK = 512
FFN_COLS = 256


def _params(*sem):
    return pltpu.CompilerParams(dimension_semantics=sem, vmem_limit_bytes=VMEM_LIMIT)


def _dot(a, b):
    return jnp.dot(a, b, preferred_element_type=F32)


def _dot_nt(a, b):
    return lax.dot_general(a, b, (((1,), (1,)), ((), ())), preferred_element_type=F32)


def _split3(x):
    h1 = x.astype(BF16)
    r1 = x - h1.astype(F32)
    h2 = r1.astype(BF16)
    h3 = (r1 - h2.astype(F32)).astype(BF16)
    return h1, h2, h3


def _dot_exact_rhs(t, x):
    h1, h2, h3 = _split3(x)
    return _dot(t, h1) + _dot(t, h2) + _dot(t, h3)


def _dot_exact_lhs(x, t):
    h1, h2, h3 = _split3(x)
    return _dot(h1, t) + _dot(h2, t) + _dot(h3, t)


def _log_sigmoid(x):
    return jnp.minimum(x, 0.0) - jnp.log(1.0 + jnp.exp(-jnp.abs(x)))


def _softplus(x):
    return jnp.maximum(x, 0.0) + jnp.log(1.0 + jnp.exp(-jnp.abs(x)))


def _sigmoid(x):
    return 1.0 / (1.0 + jnp.exp(-x))


def _silu(x):
    return x * _sigmoid(x)


def _tri(n, lower):
    r = lax.broadcasted_iota(jnp.int32, (n, n), 0)
    c = lax.broadcasted_iota(jnp.int32, (n, n), 1)
    return (c <= r) if lower else (r <= c)


def _layer_norm(v, g, b):
    mu = jnp.mean(v, axis=-1, keepdims=True)
    d = v - mu
    var = jnp.mean(d * d, axis=-1, keepdims=True)
    return d * lax.rsqrt(var + LN_EPS) * g + b


EV_MAIN = 6 * HALF
EV_GATES = 2 * ML_HEADS + FOX_HEADS
FOX_QSCALE = FOX_HD ** -0.5 * LOG2E


def _inproj_even_kernel(x_ref, wm_ref, bm_ref, wo_ref, bo_ref, wg_ref, bg_ref, wgt_ref, bgt_ref,
                        qkv_ref, mo_ref, g_ref, gt_ref):
    xb = x_ref[...].astype(BF16)
    for j in range(EV_MAIN // HALF):
        sl = slice(j * HALF, (j + 1) * HALF)
        acc = _dot(xb, wm_ref[:, sl]) + bm_ref[:, sl]
        if j == 3:
            acc = acc * FOX_QSCALE
        qkv_ref[:, sl] = acc.astype(BF16)
    mo_ref[...] = _dot(xb, wo_ref[...]) + bo_ref[...]
    g_ref[...] = _dot(xb, wg_ref[...]) + bg_ref[...]
    gt_ref[...] = _dot_nt(wgt_ref[...], xb) + bgt_ref[...]


def _inproj_even(x, w_in, b_in):
    m = x.shape[0]
    tm = ROW_TILE
    o = [0, HALF, 2 * HALF, 3 * HALF, 4 * HALF, 4 * HALF + ML_HEADS, 4 * HALF + 2 * ML_HEADS]
    fq0 = o[6]
    ff0 = fq0 + 3 * HALF
    wm = jnp.concatenate([w_in[:, :3 * HALF], w_in[:, fq0:ff0]], axis=1).astype(BF16)
    bm = jnp.concatenate([b_in[:3 * HALF], b_in[fq0:ff0]])[None, :]
    wo = w_in[:, 3 * HALF:4 * HALF].astype(BF16)
    bo = b_in[3 * HALF:4 * HALF][None, :]
    wg16 = jnp.concatenate([w_in[:, o[4]:o[6]], w_in[:, ff0:]], axis=1)
    bg16 = jnp.concatenate([b_in[o[4]:o[6]], b_in[ff0:]])
    wg = jnp.pad(wg16, ((0, 0), (0, LANES - EV_GATES))).astype(BF16)
    bg = jnp.pad(bg16, (0, LANES - EV_GATES))[None, :]
    wgt = wg16.T.astype(BF16)
    bgt = bg16[:, None]
    full = lambda shape: pl.BlockSpec(shape, lambda i: (0, 0))
    return pl.pallas_call(
        _inproj_even_kernel,
        grid=(m // tm,),
        in_specs=[pl.BlockSpec((tm, D_MODEL), lambda i: (i, 0)),
                  full((D_MODEL, EV_MAIN)), full((1, EV_MAIN)),
                  full((D_MODEL, HALF)), full((1, HALF)),
                  full((D_MODEL, LANES)), full((1, LANES)),
                  full((EV_GATES, D_MODEL)), full((EV_GATES, 1))],
        out_specs=[pl.BlockSpec((tm, EV_MAIN), lambda i: (i, 0)),
                   pl.BlockSpec((tm, HALF), lambda i: (i, 0)),
                   pl.BlockSpec((tm, LANES), lambda i: (i, 0)),
                   pl.BlockSpec((EV_GATES, tm), lambda i: (0, i))],
        out_shape=[jax.ShapeDtypeStruct((m, EV_MAIN), BF16),
                   jax.ShapeDtypeStruct((m, HALF), F32),
                   jax.ShapeDtypeStruct((m, LANES), F32),
                   jax.ShapeDtypeStruct((EV_GATES, m), F32)],
        compiler_params=_params("arbitrary"),
        name="inproj_even",
    )(x, wm, bm, wo, bo, wg, bg, wgt, bgt)


ML_LOG_SCALE = math.log(ML_HD ** -0.5)


def _mlstm_kernel(q_ref, k_ref, v_ref, mo_ref, g_ref, gt_ref, gain_ref, h_ref, f_ref,
                  ct_ref, m_ref, fcar_ref, *, lc):
    c = pl.program_id(1)

    @pl.when(c == 0)
    def _():
        ct_ref[...] = jnp.zeros_like(ct_ref)
        m_ref[...] = jnp.zeros_like(m_ref)
        fcar_ref[...] = jnp.zeros_like(fcar_ref)

    tril = _tri(lc, True)
    tril_b = jnp.where(tril, 1.0, 0.0).astype(BF16)
    triu_b = jnp.where(_tri(lc, False), 1.0, 0.0).astype(BF16)
    g = g_ref[...]
    gt = gt_ref[...]
    bc_col = _dot_exact_rhs(tril_b, _log_sigmoid(g))
    bc_row = _dot_exact_lhs(_log_sigmoid(gt), triu_b)

    f_blk = bc_row + fcar_ref[:, 0:1]
    f_ref[0] = f_blk[2 * ML_HEADS:, :] * LOG2E
    fcar_ref[...] = jnp.broadcast_to(f_blk[:, lc - 1:lc], fcar_ref.shape)

    lane = lax.broadcasted_iota(jnp.int32, (lc, ML_HD), 1)
    ones_col = jnp.where(lane == 0, 1.0, 0.0).astype(BF16)
    for h in range(ML_HEADS):
        hs = slice(h * ML_HD, (h + 1) * ML_HD)
        qh = q_ref[:, hs]
        kh = k_ref[:, hs]
        vext = jnp.concatenate([v_ref[:, hs], ones_col], axis=1)
        bcl = bc_col[:, ML_HEADS + h:ML_HEADS + h + 1]
        iil = g[:, h:h + 1]
        bcr = bc_row[ML_HEADS + h:ML_HEADS + h + 1, :]
        iir = gt[h:h + 1, :]
        m_st = m_ref[h:h + 1, 0:1]
        a_col = bcl + m_st
        dmat = jnp.where(tril, bcl + (iir - bcr), NEG)
        mt = jnp.maximum(a_col, jnp.max(dmat, axis=1, keepdims=True))
        mts = mt - ML_LOG_SCALE
        w_intra = jnp.exp(dmat - mts)
        w_inter = jnp.exp(a_col - mts)
        s = _dot_nt(qh, kh)
        intra = _dot((s * w_intra).astype(BF16), vext)
        ct = ct_ref[h]
        tot = intra + w_inter * _dot(qh, ct.astype(BF16))
        num = tot[:, :ML_HD]
        den = tot[:, ML_HD:ML_HD + 1]
        hh = num / jnp.maximum(jnp.abs(den), jnp.exp(-mt))
        mu = jnp.mean(hh, axis=-1, keepdims=True)
        d = hh - mu
        var = jnp.mean(d * d, axis=-1, keepdims=True)
        hn = d * lax.rsqrt(var + LN_EPS) * gain_ref[:, hs]
        h_ref[:, hs] = (_sigmoid(mo_ref[:, hs]) * hn).astype(BF16)

        btot = bcr[:, lc - 1:lc]
        m_new = jnp.maximum(btot + m_st, jnp.max(btot - bcr + iir, axis=1, keepdims=True))
        decay = jnp.exp(btot + m_st - m_new)
        ws = jnp.exp(btot - bcl + iil - m_new)
        kwt = (kh.astype(F32) * ws).T.astype(BF16)
        ct_ref[h] = decay * ct + _dot(kwt, vext)
        m_ref[h:h + 1, :] = jnp.broadcast_to(m_new, (1, LANES))


def _mlstm(qkv, mo, g, gt, gain, batch, seq):
    lc = min(ML_CHUNK, seq)
    nc = seq // lc
    m = batch * seq
    row = lambda j: pl.BlockSpec((lc, HALF), lambda b, c, j=j: (b * nc + c, j))
    return pl.pallas_call(
        functools.partial(_mlstm_kernel, lc=lc),
        grid=(batch, nc),
        in_specs=[row(0), row(1), row(2),
                  pl.BlockSpec((lc, HALF), lambda b, c: (b * nc + c, 0)),
                  pl.BlockSpec((lc, LANES), lambda b, c: (b * nc + c, 0)),
                  pl.BlockSpec((EV_GATES, lc), lambda b, c: (0, b * nc + c)),
                  pl.BlockSpec((1, HALF), lambda b, c: (0, 0))],
        out_specs=[pl.BlockSpec((lc, HALF), lambda b, c: (b * nc + c, 0)),
                   pl.BlockSpec((1, FOX_HEADS, lc), lambda b, c: (b, 0, c))],
        out_shape=[jax.ShapeDtypeStruct((m, HALF), BF16),
                   jax.ShapeDtypeStruct((batch, FOX_HEADS, seq), F32)],
        scratch_shapes=[pltpu.VMEM((ML_HEADS, ML_HD, 2 * ML_HD), F32),
                        pltpu.VMEM((SUBLANES, LANES), F32),
                        pltpu.VMEM((EV_GATES, LANES), F32)],
        compiler_params=_params("arbitrary", "arbitrary"),
        name="mlstm",
    )(qkv, qkv, qkv, mo, g, gt, gain)


def _fox_kernel(q_ref, k_ref, v_ref, f_ref, o_ref, m_sc, l_sc, acc_sc, *, blk):
    qi = pl.program_id(2)
    q2 = q_ref[...]
    left = lax.broadcasted_iota(jnp.int32, (blk, LANES), 1) < FOX_HD
    zero = jnp.zeros_like(q2)
    qs = (jnp.where(left, q2, zero), jnp.where(left, zero, q2))
    q0 = pl.multiple_of(qi * blk, blk)
    f0 = f_ref[0, 0, :, pl.ds(q0, LANES)][:, 0:1]

    m_sc[...] = jnp.full_like(m_sc, NEG)
    l_sc[...] = jnp.zeros_like(l_sc)
    acc_sc[...] = jnp.zeros_like(acc_sc)
    reps = blk // LANES

    def block(kj, masked):
        k0 = pl.multiple_of(kj * blk, blk)
        k2 = k_ref[pl.ds(k0, blk), :]
        v2 = v_ref[pl.ds(k0, blk), :]
        kb = f_ref[0, 0, :, pl.ds(k0, blk)] - f0
        if masked:
            keep = _tri(blk, True)
        pvs = []
        alphas = []
        for hd in range(2):
            s = _dot_nt(qs[hd], k2) - kb[hd:hd + 1, :]
            if masked:
                s = jnp.where(keep, s, NEG)
            m_prev = m_sc[hd]
            m_next = jnp.maximum(m_prev, jnp.max(s, axis=1, keepdims=True))
            alpha = jnp.exp2(m_prev - m_next)
            p = jnp.exp2(s - pltpu.repeat(m_next, reps, axis=1))
            l_sc[hd] = alpha * l_sc[hd] + jnp.sum(p, axis=1, keepdims=True)
            m_sc[hd] = m_next
            pvs.append(_dot(p.astype(BF16), v2))
            alphas.append(alpha)
        acc_sc[...] = jnp.where(left, alphas[0], alphas[1]) * acc_sc[...] + jnp.where(left, pvs[0], pvs[1])

    def body(kj, carry):
        block(kj, False)
        return carry

    lax.fori_loop(0, qi, body, 0)
    block(qi, True)
    o_ref[...] = (acc_sc[...] / jnp.where(left, l_sc[0], l_sc[1])).astype(BF16)


def _fox(qkv, f2, batch, seq):
    blk = min(FOX_BLOCK, seq)
    nq = seq // blk
    m = batch * seq
    pairs = FOX_HEADS // 2
    f4 = f2.reshape(batch, pairs, 2, seq)
    col0 = 3 * HALF // LANES
    return pl.pallas_call(
        functools.partial(_fox_kernel, blk=blk),
        grid=(batch, pairs, nq),
        in_specs=[pl.BlockSpec((blk, LANES), lambda b, p, i: (b * nq + i, col0 + p)),
                  pl.BlockSpec((seq, LANES), lambda b, p, i: (b, col0 + pairs + p)),
                  pl.BlockSpec((seq, LANES), lambda b, p, i: (b, col0 + 2 * pairs + p)),
                  pl.BlockSpec((1, 1, 2, seq), lambda b, p, i: (b, p, 0, 0))],
        out_specs=pl.BlockSpec((blk, LANES), lambda b, p, i: (b * nq + i, p)),
        out_shape=jax.ShapeDtypeStruct((m, HALF), BF16),
        scratch_shapes=[pltpu.VMEM((2, blk, LANES), F32),
                        pltpu.VMEM((2, blk, LANES), F32),
                        pltpu.VMEM((blk, LANES), F32)],
        compiler_params=_params("arbitrary", "arbitrary", "arbitrary"),
        name="fox_attention",
    )(qkv, qkv, qkv, f4)


def _outproj_kernel(a_ref, b_ref, x_ref, w_ref, g_ref, beta_ref, o_ref):
    ab = jnp.concatenate([a_ref[...], b_ref[...]], axis=1)
    y = _dot(ab, w_ref[...])
    o_ref[...] = _layer_norm(ALPHA * x_ref[...] + y, g_ref[...], beta_ref[...])


def _outproj_ln(a, b, x, w_out, ln_g, ln_b):
    m = x.shape[0]
    tm = ROW_TILE
    row = lambda width: pl.BlockSpec((tm, width), lambda i: (i, 0))
    full = lambda shape: pl.BlockSpec(shape, lambda i: (0, 0))
    return pl.pallas_call(
        _outproj_kernel,
        grid=(m // tm,),
        in_specs=[row(HALF), row(HALF), row(D_MODEL), full((D_MODEL, D_MODEL)),
                  full((1, D_MODEL)), full((1, D_MODEL))],
        out_specs=row(D_MODEL),
        out_shape=jax.ShapeDtypeStruct((m, D_MODEL), F32),
        compiler_params=_params("arbitrary"),
        name="outproj_ln",
    )(a, b, x, w_out.astype(BF16), ln_g[None, :], ln_b[None, :])


def _ffn_kernel(x_ref, wu_ref, cw_ref, cb_ref, wd_ref, g_ref, beta_ref, o_ref,
                he_ref, hprev_ref, act_ref, *, tm, tiles_per_seq):
    i = pl.program_id(0)

    @pl.when(i % tiles_per_seq == 0)
    def _():
        hprev_ref[...] = jnp.zeros_like(hprev_ref)

    xb = x_ref[...].astype(BF16)
    cw = FFN_COLS

    def conv_cols(cs):
        he_ref[0:SUBLANES, :] = hprev_ref[:, cs]
        he_ref[SUBLANES:, :] = _dot(xb, wu_ref[:, cs])
        hprev_ref[:, cs] = he_ref[tm:tm + SUBLANES, :]
        conv = cb_ref[:, cs]
        for t in range(FFN_CONV):
            off = SUBLANES - (FFN_CONV - 1) + t
            conv = conv + cw_ref[t:t + 1, cs] * he_ref[off:off + tm, :]
        return conv

    for j in range(D_FF // cw):
        val = conv_cols(slice(j * cw, (j + 1) * cw))
        gate = conv_cols(slice(D_FF + j * cw, D_FF + (j + 1) * cw))
        act_ref[:, j * cw:(j + 1) * cw] = (_silu(gate) * val).astype(BF16)
    f = _dot(act_ref[...], wd_ref[...])
    o_ref[...] = _layer_norm(ALPHA * x_ref[...] + f, g_ref[...], beta_ref[...])


def _ffn(x, w_up, conv_w, conv_b, w_down, ln_g, ln_b, seq):
    m = x.shape[0]
    tm = min(ROW_TILE, seq)
    full = lambda shape: pl.BlockSpec(shape, lambda i: (0, 0))
    return pl.pallas_call(
        functools.partial(_ffn_kernel, tm=tm, tiles_per_seq=seq // tm),
        grid=(m // tm,),
        in_specs=[pl.BlockSpec((tm, D_MODEL), lambda i: (i, 0)),
                  full((D_MODEL, 2 * D_FF)), full((FFN_CONV, 2 * D_FF)), full((1, 2 * D_FF)),
                  full((D_FF, D_MODEL)), full((1, D_MODEL)), full((1, D_MODEL))],
        out_specs=pl.BlockSpec((tm, D_MODEL), lambda i: (i, 0)),
        out_shape=jax.ShapeDtypeStruct((m, D_MODEL), F32),
        scratch_shapes=[pltpu.VMEM((tm + SUBLANES, FFN_COLS), F32),
                        pltpu.VMEM((SUBLANES, 2 * D_FF), F32),
                        pltpu.VMEM((tm, D_FF), BF16)],
        compiler_params=_params("arbitrary"),
        name="conv_ffn",
    )(x, w_up.astype(BF16), conv_w, conv_b[None, :], w_down.astype(BF16), ln_g[None, :], ln_b[None, :])


OD_MAIN = SSM_CONV_CH + 2 * HALF


def _inproj_odd_kernel(x_ref, wm_ref, wd_ref, wdt_ref, u_ref, dt_ref, dtt_ref):
    xb = x_ref[...].astype(BF16)
    for j in range(OD_MAIN // HALF):
        sl = slice(j * HALF, (j + 1) * HALF)
        u_ref[:, sl] = _dot(xb, wm_ref[:, sl])
    dt_ref[...] = _dot(xb, wd_ref[...])
    dtt_ref[...] = _dot_nt(wdt_ref[...], xb)


def _inproj_odd(x, w_in):
    m = x.shape[0]
    tm = ROW_TILE
    z1 = HALF
    x1 = z1 + SSM_CONV_CH
    d1 = x1 + SSM_HEADS
    wm = jnp.concatenate([w_in[:, z1:x1], w_in[:, :z1], w_in[:, d1:]], axis=1).astype(BF16)
    wd8 = w_in[:, x1:d1]
    wd = jnp.pad(wd8, ((0, 0), (0, LANES - SSM_HEADS))).astype(BF16)
    wdt = wd8.T.astype(BF16)
    full = lambda shape: pl.BlockSpec(shape, lambda i: (0, 0))
    return pl.pallas_call(
        _inproj_odd_kernel,
        grid=(m // tm,),
        in_specs=[pl.BlockSpec((tm, D_MODEL), lambda i: (i, 0)),
                  full((D_MODEL, OD_MAIN)), full((D_MODEL, LANES)), full((SSM_HEADS, D_MODEL))],
        out_specs=[pl.BlockSpec((tm, OD_MAIN), lambda i: (i, 0)),
                   pl.BlockSpec((tm, LANES), lambda i: (i, 0)),
                   pl.BlockSpec((SSM_HEADS, tm), lambda i: (0, i))],
        out_shape=[jax.ShapeDtypeStruct((m, OD_MAIN), F32),
                   jax.ShapeDtypeStruct((m, LANES), F32),
                   jax.ShapeDtypeStruct((SSM_HEADS, m), F32)],
        compiler_params=_params("arbitrary"),
        name="inproj_odd",
    )(x, wm, wd, wdt)


SSM_HPG = SSM_HEADS // SSM_GROUPS
GROUP_W = SSM_HPG * SSM_HEADDIM


def _ssd_kernel(xbc_ref, z_ref, dt_ref, dtt_ref, cw_ref, cb_ref, dtb_ref, dtbt_ref, a_ref, at_ref,
                dskip_ref, gain_ref, y_ref, xe_ref, ht_ref, *, lc):
    c = pl.program_id(1)

    @pl.when(c == 0)
    def _():
        xe_ref[0:SUBLANES, :] = jnp.zeros((SUBLANES, SSM_CONV_CH), F32)
        ht_ref[...] = jnp.zeros_like(ht_ref)

    xe_ref[SUBLANES:, :] = xbc_ref[...]
    conv = cb_ref[...]
    for t in range(SSM_CONV):
        off = SUBLANES - (SSM_CONV - 1) + t
        conv = conv + cw_ref[t:t + 1, :] * xe_ref[off:off + lc, :]
    xe_ref[0:SUBLANES, :] = xe_ref[lc:lc + SUBLANES, :]
    xact = _silu(conv)
    xs = xact[:, :HALF]

    tril = _tri(lc, True)
    tril_b = jnp.where(tril, 1.0, 0.0).astype(BF16)
    triu_b = jnp.where(_tri(lc, False), 1.0, 0.0).astype(BF16)
    er = lax.broadcasted_iota(jnp.int32, (LANES, HALF), 0)
    ec = lax.broadcasted_iota(jnp.int32, (LANES, HALF), 1)
    expand = jnp.where(jnp.right_shift(ec, 6) == er, 1.0, 0.0).astype(BF16)

    dt_col = _softplus(dt_ref[...] + dtb_ref[...])
    dt_row = _softplus(dtt_ref[...] + dtbt_ref[...])
    acs_col = _dot_exact_rhs(tril_b, dt_col * a_ref[...])
    acs_row = _dot_exact_lhs(dt_row * at_ref[...], triu_b)
    acs_x = _dot_exact_lhs(acs_col, expand)
    dt_x = _dot_exact_lhs(dt_col, expand)
    xdt = xs * dt_x
    left = lax.broadcasted_iota(jnp.int32, (lc, LANES), 1) < SSM_HEADDIM

    ys = []
    for gi in range(SSM_GROUPS):
        gs = slice(gi * GROUP_W, (gi + 1) * GROUP_W)
        bm = xact[:, HALF + gi * SSM_STATE:HALF + (gi + 1) * SSM_STATE]
        cm = xact[:, HALF + (SSM_GROUPS + gi) * SSM_STATE:HALF + (SSM_GROUPS + gi + 1) * SSM_STATE]
        cmb = cm.astype(BF16)
        cbm = _dot_nt(cmb, bm.astype(BF16))
        pair_out = []
        for pj in range(SSM_HPG // 2):
            lo = gi * GROUP_W + pj * LANES
            xpair = xdt[:, lo:lo + LANES].astype(BF16)
            yh = []
            for e in (gi * SSM_HPG + 2 * pj, gi * SSM_HPG + 2 * pj + 1):
                seg = jnp.where(tril, acs_col[:, e:e + 1] - acs_row[e:e + 1, :], NEG)
                yh.append(_dot((cbm * jnp.exp(seg)).astype(BF16), xpair))
            pair_out.append(jnp.where(left, yh[0], yh[1]))
        y_diag = jnp.concatenate(pair_out, axis=1)
        ht = ht_ref[gi]
        acs_g = acs_x[:, gs]
        y_off = _dot(cmb, ht.astype(BF16)) * jnp.exp(acs_g)
        ys.append(y_diag + y_off)
        last = acs_g[lc - 1:lc, :]
        xw = (xdt[:, gs] * jnp.exp(last - acs_g)).astype(BF16)
        ht_ref[gi] = jnp.exp(last) * ht + _dot(bm.T.astype(BF16), xw)
    y = jnp.concatenate(ys, axis=1) + dskip_ref[...] * xs
    gated = y * _silu(z_ref[...])
    ms = jnp.mean(gated * gated, axis=-1, keepdims=True)
    y_ref[...] = (gated * lax.rsqrt(ms + LN_EPS) * gain_ref[...]).astype(BF16)


def _ssd(u, dt, dtt, conv_w, conv_b, dt_bias, a_log, d_skip, ssm_norm, batch, seq):
    lc = min(SSD_CHUNK, seq)
    nc = seq // lc
    m = batch * seq
    a = -jnp.exp(a_log.astype(F32))
    pad = lambda v: jnp.pad(v, (0, LANES - SSM_HEADS))[None, :]
    full = lambda shape: pl.BlockSpec(shape, lambda b, c: (0, 0))
    return pl.pallas_call(
        functools.partial(_ssd_kernel, lc=lc),
        grid=(batch, nc),
        in_specs=[pl.BlockSpec((lc, SSM_CONV_CH), lambda b, c: (b * nc + c, 0)),
                  pl.BlockSpec((lc, HALF), lambda b, c: (b * nc + c, SSM_CONV_CH // HALF)),
                  pl.BlockSpec((lc, LANES), lambda b, c: (b * nc + c, 0)),
                  pl.BlockSpec((SSM_HEADS, lc), lambda b, c: (0, b * nc + c)),
                  full((SSM_CONV, SSM_CONV_CH)), full((1, SSM_CONV_CH)),
                  full((1, LANES)), full((SSM_HEADS, 1)), full((1, LANES)), full((SSM_HEADS, 1)),
                  full((1, HALF)), full((1, HALF))],
        out_specs=pl.BlockSpec((lc, HALF), lambda b, c: (b * nc + c, 0)),
        out_shape=jax.ShapeDtypeStruct((m, HALF), BF16),
        scratch_shapes=[pltpu.VMEM((lc + SUBLANES, SSM_CONV_CH), F32),
                        pltpu.VMEM((SSM_GROUPS, SSM_STATE, GROUP_W), F32)],
        compiler_params=_params("arbitrary", "arbitrary"),
        name="ssd",
    )(u, u, dt, dtt, conv_w, conv_b[None, :], pad(dt_bias), dt_bias[:, None], pad(a), a[:, None],
      jnp.repeat(d_skip, SSM_HEADDIM)[None, :], ssm_norm[None, :])


POOL_HALO = 16


def _pool_kernel(u_ref, w_ref, b_ref, sc_ref, p_ref, ue_ref, *, tm, tiles_per_seq):
    i = pl.program_id(0)

    @pl.when(i % tiles_per_seq == 0)
    def _():
        ue_ref[0:POOL_HALO, :] = jnp.zeros((POOL_HALO, HALF), F32)

    ue_ref[POOL_HALO:, :] = u_ref[...]
    t = (i % tiles_per_seq) * tm + lax.broadcasted_iota(jnp.int32, (tm, 1), 0)
    for gi, win in enumerate(POOL_WINDOWS):
        gs = slice(gi * POOL_GW, (gi + 1) * POOL_GW)
        wsum = ue_ref[POOL_HALO:POOL_HALO + tm, gs]
        for j in range(1, win):
            wsum = wsum + ue_ref[POOL_HALO - j:POOL_HALO - j + tm, gs]
        cnt = jnp.minimum(t + 1, win).astype(F32)
        pooled = wsum / cnt - ue_ref[POOL_HALO:POOL_HALO + tm, gs]
        y = _dot(pooled.astype(BF16), w_ref[gi]) + b_ref[:, gs]
        p_ref[:, gs] = (y * sc_ref[:, gs]).astype(BF16)
    ue_ref[0:POOL_HALO, :] = ue_ref[tm:tm + POOL_HALO, :]


def _pool(u, pool_w, pool_b, pool_scale, seq):
    m = u.shape[0]
    tm = min(ROW_TILE, seq)
    full2 = lambda shape: pl.BlockSpec(shape, lambda i: (0, 0))
    return pl.pallas_call(
        functools.partial(_pool_kernel, tm=tm, tiles_per_seq=seq // tm),
        grid=(m // tm,),
        in_specs=[pl.BlockSpec((tm, HALF), lambda i: (i, (SSM_CONV_CH + HALF) // HALF)),
                  pl.BlockSpec((len(POOL_WINDOWS), POOL_GW, POOL_GW), lambda i: (0, 0, 0)),
                  full2((1, HALF)), full2((1, HALF))],
        out_specs=pl.BlockSpec((tm, HALF), lambda i: (i, 0)),
        out_shape=jax.ShapeDtypeStruct((m, HALF), BF16),
        scratch_shapes=[pltpu.VMEM((tm + POOL_HALO, HALF), F32)],
        compiler_params=_params("arbitrary"),
        name="pool",
    )(u, pool_w.astype(BF16), pool_b[None, :], pool_scale[None, :])


def kernel(x, ev_w_in, ev_b_in, ev_ml_norm, ev_w_out, od_w_in, od_conv_w, od_conv_b, od_dt_bias, od_a_log,
           od_d_skip, od_ssm_norm, od_pool_w, od_pool_b, od_pool_scale, od_w_out, ffn_w_up, ffn_conv_w,
           ffn_conv_b, ffn_w_down, ln1_g, ln1_b, ln2_g, ln2_b):
    batch, seq, _ = x.shape
    h = x.reshape(batch * seq, D_MODEL)
    for layer in range(DEPTH):
        j = layer // 2
        if layer % 2 == 0:
            qkv, mo, g, gt = _inproj_even(h, ev_w_in[j], ev_b_in[j])
            a, f2 = _mlstm(qkv, mo, g, gt, ev_ml_norm[j][None, :], batch, seq)
            b = _fox(qkv, f2, batch, seq)
            w_out = ev_w_out[j]
        else:
            u, dt, dtt = _inproj_odd(h, od_w_in[j])
            a = _ssd(u, dt, dtt, od_conv_w[j], od_conv_b[j], od_dt_bias[j], od_a_log[j], od_d_skip[j],
                     od_ssm_norm[j], batch, seq)
            b = _pool(u, od_pool_w[j], od_pool_b[j], od_pool_scale[j], seq)
            w_out = od_w_out[j]
        h = _outproj_ln(a, b, h, w_out, ln1_g[layer], ln1_b[layer])
        h = _ffn(h, ffn_w_up[layer], ffn_conv_w[layer], ffn_conv_b[layer], ffn_w_down[layer],
                 ln2_g[layer], ln2_b[layer], seq)
    return h.reshape(batch, seq, D_MODEL)
```

```python
import functools
import math

import jax
import jax.numpy as jnp
import numpy as np
from jax import lax
from jax.experimental import pallas as pl
from jax.experimental.pallas import tpu as pltpu

F32 = jnp.float32
BF16 = jnp.bfloat16

D_MODEL = 1024
DEPTH = 4
HALF = D_MODEL // 2
ML_HEADS = 4
ML_HD = HALF // ML_HEADS
FOX_HEADS = 8
FOX_HD = HALF // FOX_HEADS
SSM_HEADDIM = 64
SSM_HEADS = HALF // SSM_HEADDIM
SSM_GROUPS = 2
SSM_STATE = 128
SSM_CONV = 4
SSM_CONV_CH = HALF + 2 * SSM_GROUPS * SSM_STATE
POOL_WINDOWS = (2, 4, 8, 16)
POOL_GW = HALF // len(POOL_WINDOWS)
D_FF = 256 * ((8 * D_MODEL // 3 + 255) // 256)
FFN_CONV = 3
LN_EPS = 1e-5
ALPHA = (2.0 * DEPTH) ** 0.25

LANES = 128
SUBLANES = 8
NEG = -1e30
LOG2E = 1.4426950408889634
VMEM_LIMIT = 56 * 1024 * 1024

ROW_TILE = 512
ML_CHUNK = 256
SSD_CHUNK = 256
FOX_BLOCK = 512
FOX_UNROLL = 4
FFN_COLS = 256


def _params(*sem):
    return pltpu.CompilerParams(dimension_semantics=sem, vmem_limit_bytes=VMEM_LIMIT)


def _dot(a, b):
    return jnp.dot(a, b, preferred_element_type=F32)


def _dot_nt(a, b):
    return lax.dot_general(a, b, (((1,), (1,)), ((), ())), preferred_element_type=F32)


def _split3(x):
    h1 = x.astype(BF16)
    r1 = x - h1.astype(F32)
    h2 = r1.astype(BF16)
    h3 = (r1 - h2.astype(F32)).astype(BF16)
    return h1, h2, h3


def _dot_exact_rhs(t, x):
    h1, h2, h3 = _split3(x)
    return _dot(t, h1) + _dot(t, h2) + _dot(t, h3)


def _dot_exact_lhs(x, t):
    h1, h2, h3 = _split3(x)
    return _dot(h1, t) + _dot(h2, t) + _dot(h3, t)


def _log_sigmoid(x):
    return jnp.minimum(x, 0.0) - jnp.log(1.0 + jnp.exp(-jnp.abs(x)))


def _softplus(x):
    return jnp.maximum(x, 0.0) + jnp.log(1.0 + jnp.exp(-jnp.abs(x)))


def _sigmoid(x):
    return 1.0 / (1.0 + jnp.exp(-x))


def _silu(x):
    return x * _sigmoid(x)


def _tri(n, lower):
    r = lax.broadcasted_iota(jnp.int32, (n, n), 0)
    c = lax.broadcasted_iota(jnp.int32, (n, n), 1)
    return (c <= r) if lower else (r <= c)


def _layer_norm(v, g, b):
    mu = jnp.mean(v, axis=-1, keepdims=True)
    d = v - mu
    var = jnp.mean(d * d, axis=-1, keepdims=True)
    return d * lax.rsqrt(var + LN_EPS) * g + b


EV_MAIN = 5 * HALF
EV_GATES = 2 * ML_HEADS + FOX_HEADS
FOX_QSCALE = FOX_HD ** -0.5 * LOG2E


def _inproj_even_kernel(x_ref, wm_ref, bm_ref, wo_ref, bo_ref, wg_ref, bg_ref, wgt_ref, bgt_ref,
                        wvt_ref, bvt_ref, qkv_ref, mo_ref, g_ref, gt_ref, vt_ref):
    xb = x_ref[...].astype(BF16)
    for j in range(EV_MAIN // HALF):
        sl = slice(j * HALF, (j + 1) * HALF)
        acc = _dot(xb, wm_ref[:, sl]) + bm_ref[:, sl]
        if j == 3:
            acc = acc * FOX_QSCALE
        qkv_ref[:, sl] = acc.astype(BF16)
    mo_ref[...] = _dot(xb, wo_ref[...]) + bo_ref[...]
    g_ref[...] = _dot(xb, wg_ref[...]) + bg_ref[...]
    gt_ref[...] = _dot_nt(wgt_ref[...], xb) + bgt_ref[...]
    vt_ref[...] = (_dot_nt(wvt_ref[...], xb) + bvt_ref[...]).astype(BF16)


def _inproj_even(x, w_in, b_in):
    m = x.shape[0]
    tm = ROW_TILE
    o = [0, HALF, 2 * HALF, 3 * HALF, 4 * HALF, 4 * HALF + ML_HEADS, 4 * HALF + 2 * ML_HEADS]
    fq0 = o[6]
    fv0 = fq0 + 2 * HALF
    ff0 = fq0 + 3 * HALF
    wm = jnp.concatenate([w_in[:, :3 * HALF], w_in[:, fq0:fv0]], axis=1).astype(BF16)
    bm = jnp.concatenate([b_in[:3 * HALF], b_in[fq0:fv0]])[None, :]
    wvt = w_in[:, fv0:ff0].T.astype(BF16)
    bvt = b_in[fv0:ff0][:, None]
    wo = w_in[:, 3 * HALF:4 * HALF].astype(BF16)
    bo = b_in[3 * HALF:4 * HALF][None, :]
    wg16 = jnp.concatenate([w_in[:, o[4]:o[6]], w_in[:, ff0:]], axis=1)
    bg16 = jnp.concatenate([b_in[o[4]:o[6]], b_in[ff0:]])
    wg = jnp.pad(wg16, ((0, 0), (0, LANES - EV_GATES))).astype(BF16)
    bg = jnp.pad(bg16, (0, LANES - EV_GATES))[None, :]
    wgt = wg16.T.astype(BF16)
    bgt = bg16[:, None]
    full = lambda shape: pl.BlockSpec(shape, lambda i: (0, 0))
    return pl.pallas_call(
        _inproj_even_kernel,
        grid=(m // tm,),
        in_specs=[pl.BlockSpec((tm, D_MODEL), lambda i: (i, 0)),
                  full((D_MODEL, EV_MAIN)), full((1, EV_MAIN)),
                  full((D_MODEL, HALF)), full((1, HALF)),
                  full((D_MODEL, LANES)), full((1, LANES)),
                  full((EV_GATES, D_MODEL)), full((EV_GATES, 1)),
                  full((HALF, D_MODEL)), full((HALF, 1))],
        out_specs=[pl.BlockSpec((tm, EV_MAIN), lambda i: (i, 0)),
                   pl.BlockSpec((tm, HALF), lambda i: (i, 0)),
                   pl.BlockSpec((tm, LANES), lambda i: (i, 0)),
                   pl.BlockSpec((EV_GATES, tm), lambda i: (0, i)),
                   pl.BlockSpec((HALF, tm), lambda i: (0, i))],
        out_shape=[jax.ShapeDtypeStruct((m, EV_MAIN), BF16),
                   jax.ShapeDtypeStruct((m, HALF), F32),
                   jax.ShapeDtypeStruct((m, LANES), F32),
                   jax.ShapeDtypeStruct((EV_GATES, m), F32),
                   jax.ShapeDtypeStruct((HALF, m), BF16)],
        compiler_params=_params("arbitrary"),
        name="inproj_even",
    )(x, wm, bm, wo, bo, wg, bg, wgt, bgt, wvt, bvt)


ML_LOG_SCALE = math.log(ML_HD ** -0.5)


FOX_PAIRS = FOX_HEADS // 2
KEXT_W = 2 * LANES
BIAS_PARTS = 3


def _bias_routing():
    r = np.zeros((BIAS_PARTS, LANES, FOX_PAIRS * LANES), np.float32)
    for head in range(FOX_HEADS):
        for part in range(BIAS_PARTS):
            r[part, 2 * ML_HEADS + head, (head // 2) * LANES + BIAS_PARTS * (head % 2) + part] = 1.0
    return jnp.asarray(r, BF16)


def _mlstm_kernel(q_ref, k_ref, v_ref, fk_ref, mo_ref, g_ref, gt_ref, gain_ref, route_ref, h_ref, kext_ref,
                  ct_ref, m_ref, fcar_ref, *, lc):
    c = pl.program_id(1)

    @pl.when(c == 0)
    def _():
        ct_ref[...] = jnp.zeros_like(ct_ref)
        m_ref[...] = jnp.zeros_like(m_ref)
        fcar_ref[...] = jnp.zeros_like(fcar_ref)

    tril = _tri(lc, True)
    tril_b = jnp.where(tril, 1.0, 0.0).astype(BF16)
    triu_b = jnp.where(_tri(lc, False), 1.0, 0.0).astype(BF16)
    g = g_ref[...]
    gt = gt_ref[...]
    bc_col = _dot_exact_rhs(tril_b, _log_sigmoid(g))
    bc_row = _dot_exact_lhs(_log_sigmoid(gt), triu_b)

    f_blk = bc_col + fcar_ref[0:1, :]
    fcar_ref[...] = jnp.broadcast_to(f_blk[lc - 1:lc, :], fcar_ref.shape)
    parts = _split3(f_blk * LOG2E)
    bias = sum(_dot(parts[i], route_ref[i]) for i in range(BIAS_PARTS)).astype(BF16)
    for p in range(FOX_PAIRS):
        kext_ref[:, p * KEXT_W:p * KEXT_W + LANES] = fk_ref[:, p * LANES:(p + 1) * LANES]
        kext_ref[:, p * KEXT_W + LANES:(p + 1) * KEXT_W] = bias[:, p * LANES:(p + 1) * LANES]

    lane = lax.broadcasted_iota(jnp.int32, (lc, ML_HD), 1)
    ones_col = jnp.where(lane == 0, 1.0, 0.0).astype(BF16)
    for h in range(ML_HEADS):
        hs = slice(h * ML_HD, (h + 1) * ML_HD)
        qh = q_ref[:, hs]
        kh = k_ref[:, hs]
        vext = jnp.concatenate([v_ref[:, hs], ones_col], axis=1)
        bcl = bc_col[:, ML_HEADS + h:ML_HEADS + h + 1]
        iil = g[:, h:h + 1]
        bcr = bc_row[ML_HEADS + h:ML_HEADS + h + 1, :]
        iir = gt[h:h + 1, :]
        m_st = m_ref[h:h + 1, 0:1]
        a_col = bcl + m_st
        dmat = jnp.where(tril, bcl + (iir - bcr), NEG)
        mt = jnp.maximum(a_col, jnp.max(dmat, axis=1, keepdims=True))
        mts = mt - ML_LOG_SCALE
        w_intra = jnp.exp(dmat - mts)
        w_inter = jnp.exp(a_col - mts)
        s = _dot_nt(qh, kh)
        intra = _dot((s * w_intra).astype(BF16), vext)
        ct = ct_ref[h]
        tot = intra + w_inter * _dot(qh, ct.astype(BF16))
        num = tot[:, :ML_HD]
        den = tot[:, ML_HD:ML_HD + 1]
        hh = num / jnp.maximum(jnp.abs(den), jnp.exp(-mt))
        mu = jnp.mean(hh, axis=-1, keepdims=True)
        d = hh - mu
        var = jnp.mean(d * d, axis=-1, keepdims=True)
        hn = d * lax.rsqrt(var + LN_EPS) * gain_ref[:, hs]
        h_ref[:, hs] = (_sigmoid(mo_ref[:, hs]) * hn).astype(BF16)

        btot = bcr[:, lc - 1:lc]
        m_new = jnp.maximum(btot + m_st, jnp.max(btot - bcr + iir, axis=1, keepdims=True))
        decay = jnp.exp(btot + m_st - m_new)
        ws = jnp.exp(btot - bcl + iil - m_new)
        kwt = (kh.astype(F32) * ws).T.astype(BF16)
        ct_ref[h] = decay * ct + _dot(kwt, vext)
        m_ref[h:h + 1, :] = jnp.broadcast_to(m_new, (1, LANES))


def _mlstm(qkv, mo, g, gt, gain, batch, seq):
    lc = min(ML_CHUNK, seq)
    nc = seq // lc
    m = batch * seq
    row = lambda j: pl.BlockSpec((lc, HALF), lambda b, c, j=j: (b * nc + c, j))
    return pl.pallas_call(
        functools.partial(_mlstm_kernel, lc=lc),
        grid=(batch, nc),
        in_specs=[row(0), row(1), row(2), row(4),
                  pl.BlockSpec((lc, HALF), lambda b, c: (b * nc + c, 0)),
                  pl.BlockSpec((lc, LANES), lambda b, c: (b * nc + c, 0)),
                  pl.BlockSpec((EV_GATES, lc), lambda b, c: (0, b * nc + c)),
                  pl.BlockSpec((1, HALF), lambda b, c: (0, 0)),
                  pl.BlockSpec((BIAS_PARTS, LANES, FOX_PAIRS * LANES), lambda b, c: (0, 0, 0))],
        out_specs=[pl.BlockSpec((lc, HALF), lambda b, c: (b * nc + c, 0)),
                   pl.BlockSpec((lc, FOX_PAIRS * KEXT_W), lambda b, c: (b * nc + c, 0))],
        out_shape=[jax.ShapeDtypeStruct((m, HALF), BF16),
                   jax.ShapeDtypeStruct((m, FOX_PAIRS * KEXT_W), BF16)],
        scratch_shapes=[pltpu.VMEM((ML_HEADS, ML_HD, 2 * ML_HD), F32),
                        pltpu.VMEM((SUBLANES, LANES), F32),
                        pltpu.VMEM((SUBLANES, LANES), F32)],
        compiler_params=_params("arbitrary", "arbitrary"),
        name="mlstm",
    )(qkv, qkv, qkv, qkv, mo, g, gt, gain, _bias_routing())


def _fox_kernel(q_ref, kext_ref, vt_ref, o_ref, m_sc, l_sc, acc_sc, s0_sc, s1_sc, mx0_sc, mx1_sc, *, blk):
    qi = pl.program_id(2)
    q2 = q_ref[...]
    lane = lax.broadcasted_iota(jnp.int32, (blk, LANES), 1)
    left = lane < FOX_HD
    zero = jnp.zeros_like(q2)
    qe = []
    for hd in range(2):
        pick = (lane >= BIAS_PARTS * hd) & (lane < BIAS_PARTS * (hd + 1))
        minus1 = jnp.where(pick, -1.0, 0.0).astype(BF16)
        qh = jnp.where(left, q2, zero) if hd == 0 else jnp.where(left, zero, q2)
        qe.append(jnp.concatenate([qh, minus1], axis=1))

    m_sc[...] = jnp.full_like(m_sc, NEG)
    l_sc[...] = jnp.zeros_like(l_sc)
    acc_sc[...] = jnp.zeros_like(acc_sc)
    s_bufs = (s0_sc, s1_sc)
    mx_bufs = (mx0_sc, mx1_sc)

    def scores(kj, slot, masked):
        k0 = pl.multiple_of(kj * blk, blk)
        ke = kext_ref[pl.ds(k0, blk), :]
        for hd in range(2):
            s = _dot_nt(ke, qe[hd])
            if masked:
                s = jnp.where(_tri(blk, False), s, NEG)
            s_bufs[slot][hd] = s
            mx_bufs[slot][hd] = jnp.max(s, axis=0, keepdims=True)

    def accumulate(kj, slot):
        k0 = pl.multiple_of(kj * blk, blk)
        for hd in range(2):
            rows = slice(hd * FOX_HD, (hd + 1) * FOX_HD)
            m_prev = m_sc[hd]
            m_next = jnp.maximum(m_prev, mx_bufs[slot][hd])
            alpha = jnp.exp2(m_prev - m_next)
            p = jnp.exp2(s_bufs[slot][hd] - m_next)
            l_sc[hd] = alpha * l_sc[hd] + jnp.sum(p, axis=0, keepdims=True)
            m_sc[hd] = m_next
            pv = _dot(vt_ref[rows, pl.ds(k0, blk)], p.astype(BF16))
            acc_sc[rows, :] = alpha * acc_sc[rows, :] + pv

    def key_block(t):
        return jnp.where(t == 0, qi, t - 1)

    def step(t, slot):
        scores(t - 1, slot, False)
        accumulate(key_block(t - 1), 1 - slot)

    scores(qi, 0, True)

    def body(i, carry):
        for u in range(FOX_UNROLL):
            step(FOX_UNROLL * i + u + 1, (u + 1) % 2)
        return carry

    lax.fori_loop(0, qi // FOX_UNROLL, body, 0)
    done = (qi // FOX_UNROLL) * FOX_UNROLL
    width = FOX_UNROLL // 2
    while width >= 1:
        @pl.when(qi & width != 0)
        def _(done=done, width=width):
            for u in range(width):
                step(done + u + 1, (u + 1) % 2)
        done = done + (qi & width)
        width //= 2

    @pl.when(qi % 2 == 1)
    def _():
        accumulate(key_block(qi), 1)

    @pl.when(qi % 2 == 0)
    def _():
        accumulate(key_block(qi), 0)

    inv = jnp.concatenate([jnp.broadcast_to(1.0 / l_sc[hd], (FOX_HD, blk)) for hd in range(2)], axis=0)
    o_ref[...] = (acc_sc[...] * inv).T.astype(BF16)


def _fox(qkv, kext, vt, batch, seq):
    blk = min(FOX_BLOCK, seq)
    nq = seq // blk
    m = batch * seq
    col0 = 3 * HALF // LANES
    return pl.pallas_call(
        functools.partial(_fox_kernel, blk=blk),
        grid=(batch, FOX_PAIRS, nq),
        in_specs=[pl.BlockSpec((blk, LANES), lambda b, p, i: (b * nq + i, col0 + p)),
                  pl.BlockSpec((seq, KEXT_W), lambda b, p, i: (b, p)),
                  pl.BlockSpec((LANES, seq), lambda b, p, i: (p, b))],
        out_specs=pl.BlockSpec((blk, LANES), lambda b, p, i: (b * nq + i, p)),
        out_shape=jax.ShapeDtypeStruct((m, HALF), BF16),
        scratch_shapes=[pltpu.VMEM((2, 1, blk), F32),
                        pltpu.VMEM((2, 1, blk), F32),
                        pltpu.VMEM((LANES, blk), F32),
                        pltpu.VMEM((2, blk, blk), F32),
                        pltpu.VMEM((2, blk, blk), F32),
                        pltpu.VMEM((2, 1, blk), F32),
                        pltpu.VMEM((2, 1, blk), F32)],
        compiler_params=_params("arbitrary", "arbitrary", "arbitrary"),
        name="fox_attention",
    )(qkv, kext, vt)


def _outproj_kernel(a_ref, b_ref, x_ref, w_ref, g_ref, beta_ref, o_ref):
    ab = jnp.concatenate([a_ref[...], b_ref[...]], axis=1)
    y = _dot(ab, w_ref[...])
    o_ref[...] = _layer_norm(ALPHA * x_ref[...] + y, g_ref[...], beta_ref[...])


def _outproj_ln(a, b, x, w_out, ln_g, ln_b):
    m = x.shape[0]
    tm = ROW_TILE
    row = lambda width: pl.BlockSpec((tm, width), lambda i: (i, 0))
    full = lambda shape: pl.BlockSpec(shape, lambda i: (0, 0))
    return pl.pallas_call(
        _outproj_kernel,
        grid=(m // tm,),
        in_specs=[row(HALF), row(HALF), row(D_MODEL), full((D_MODEL, D_MODEL)),
                  full((1, D_MODEL)), full((1, D_MODEL))],
        out_specs=row(D_MODEL),
        out_shape=jax.ShapeDtypeStruct((m, D_MODEL), F32),
        compiler_params=_params("arbitrary"),
        name="outproj_ln",
    )(a, b, x, w_out.astype(BF16), ln_g[None, :], ln_b[None, :])


def _ffn_kernel(x_ref, wu_ref, cw_ref, cb_ref, wd_ref, g_ref, beta_ref, o_ref,
                he_ref, hprev_ref, act_ref, *, tm, tiles_per_seq):
    i = pl.program_id(0)

    @pl.when(i % tiles_per_seq == 0)
    def _():
        hprev_ref[...] = jnp.zeros_like(hprev_ref)

    xb = x_ref[...].astype(BF16)
    cw = FFN_COLS

    def conv_cols(cs):
        he_ref[0:SUBLANES, :] = hprev_ref[:, cs]
        he_ref[SUBLANES:, :] = _dot(xb, wu_ref[:, cs])
        hprev_ref[:, cs] = he_ref[tm:tm + SUBLANES, :]
        conv = cb_ref[:, cs]
        for t in range(FFN_CONV):
            off = SUBLANES - (FFN_CONV - 1) + t
            conv = conv + cw_ref[t:t + 1, cs] * he_ref[off:off + tm, :]
        return conv

    for j in range(D_FF // cw):
        val = conv_cols(slice(j * cw, (j + 1) * cw))
        gate = conv_cols(slice(D_FF + j * cw, D_FF + (j + 1) * cw))
        act_ref[:, j * cw:(j + 1) * cw] = (_silu(gate) * val).astype(BF16)
    f = _dot(act_ref[...], wd_ref[...])
    o_ref[...] = _layer_norm(ALPHA * x_ref[...] + f, g_ref[...], beta_ref[...])


def _ffn(x, w_up, conv_w, conv_b, w_down, ln_g, ln_b, seq):
    m = x.shape[0]
    tm = min(ROW_TILE, seq)
    full = lambda shape: pl.BlockSpec(shape, lambda i: (0, 0))
    return pl.pallas_call(
        functools.partial(_ffn_kernel, tm=tm, tiles_per_seq=seq // tm),
        grid=(m // tm,),
        in_specs=[pl.BlockSpec((tm, D_MODEL), lambda i: (i, 0)),
                  full((D_MODEL, 2 * D_FF)), full((FFN_CONV, 2 * D_FF)), full((1, 2 * D_FF)),
                  full((D_FF, D_MODEL)), full((1, D_MODEL)), full((1, D_MODEL))],
        out_specs=pl.BlockSpec((tm, D_MODEL), lambda i: (i, 0)),
        out_shape=jax.ShapeDtypeStruct((m, D_MODEL), F32),
        scratch_shapes=[pltpu.VMEM((tm + SUBLANES, FFN_COLS), F32),
                        pltpu.VMEM((SUBLANES, 2 * D_FF), F32),
                        pltpu.VMEM((tm, D_FF), BF16)],
        compiler_params=_params("arbitrary"),
        name="conv_ffn",
    )(x, w_up.astype(BF16), conv_w, conv_b[None, :], w_down.astype(BF16), ln_g[None, :], ln_b[None, :])


OD_MAIN = SSM_CONV_CH + 2 * HALF


def _inproj_odd_kernel(x_ref, wm_ref, wd_ref, wdt_ref, u_ref, dt_ref, dtt_ref):
    xb = x_ref[...].astype(BF16)
    for j in range(OD_MAIN // HALF):
        sl = slice(j * HALF, (j + 1) * HALF)
        u_ref[:, sl] = _dot(xb, wm_ref[:, sl])
    dt_ref[...] = _dot(xb, wd_ref[...])
    dtt_ref[...] = _dot_nt(wdt_ref[...], xb)


def _inproj_odd(x, w_in):
    m = x.shape[0]
    tm = ROW_TILE
    z1 = HALF
    x1 = z1 + SSM_CONV_CH
    d1 = x1 + SSM_HEADS
    wm = jnp.concatenate([w_in[:, z1:x1], w_in[:, :z1], w_in[:, d1:]], axis=1).astype(BF16)
    wd8 = w_in[:, x1:d1]
    wd = jnp.pad(wd8, ((0, 0), (0, LANES - SSM_HEADS))).astype(BF16)
    wdt = wd8.T.astype(BF16)
    full = lambda shape: pl.BlockSpec(shape, lambda i: (0, 0))
    return pl.pallas_call(
        _inproj_odd_kernel,
        grid=(m // tm,),
        in_specs=[pl.BlockSpec((tm, D_MODEL), lambda i: (i, 0)),
                  full((D_MODEL, OD_MAIN)), full((D_MODEL, LANES)), full((SSM_HEADS, D_MODEL))],
        out_specs=[pl.BlockSpec((tm, OD_MAIN), lambda i: (i, 0)),
                   pl.BlockSpec((tm, LANES), lambda i: (i, 0)),
                   pl.BlockSpec((SSM_HEADS, tm), lambda i: (0, i))],
        out_shape=[jax.ShapeDtypeStruct((m, OD_MAIN), F32),
                   jax.ShapeDtypeStruct((m, LANES), F32),
                   jax.ShapeDtypeStruct((SSM_HEADS, m), F32)],
        compiler_params=_params("arbitrary"),
        name="inproj_odd",
    )(x, wm, wd, wdt)


SSM_HPG = SSM_HEADS // SSM_GROUPS
GROUP_W = SSM_HPG * SSM_HEADDIM


def _ssd_kernel(xbc_ref, z_ref, dt_ref, dtt_ref, cw_ref, cb_ref, dtb_ref, dtbt_ref, a_ref, at_ref,
                dskip_ref, gain_ref, y_ref, xe_ref, ht_ref, *, lc):
    c = pl.program_id(1)

    @pl.when(c == 0)
    def _():
        xe_ref[0:SUBLANES, :] = jnp.zeros((SUBLANES, SSM_CONV_CH), F32)
        ht_ref[...] = jnp.zeros_like(ht_ref)

    xe_ref[SUBLANES:, :] = xbc_ref[...]
    conv = cb_ref[...]
    for t in range(SSM_CONV):
        off = SUBLANES - (SSM_CONV - 1) + t
        conv = conv + cw_ref[t:t + 1, :] * xe_ref[off:off + lc, :]
    xe_ref[0:SUBLANES, :] = xe_ref[lc:lc + SUBLANES, :]
    xact = _silu(conv)
    xs = xact[:, :HALF]

    tril = _tri(lc, True)
    tril_b = jnp.where(tril, 1.0, 0.0).astype(BF16)
    triu_b = jnp.where(_tri(lc, False), 1.0, 0.0).astype(BF16)
    er = lax.broadcasted_iota(jnp.int32, (LANES, HALF), 0)
    ec = lax.broadcasted_iota(jnp.int32, (LANES, HALF), 1)
    expand = jnp.where(jnp.right_shift(ec, 6) == er, 1.0, 0.0).astype(BF16)

    dt_col = _softplus(dt_ref[...] + dtb_ref[...])
    dt_row = _softplus(dtt_ref[...] + dtbt_ref[...])
    acs_col = _dot_exact_rhs(tril_b, dt_col * a_ref[...])
    acs_row = _dot_exact_lhs(dt_row * at_ref[...], triu_b)
    acs_x = _dot_exact_lhs(acs_col, expand)
    dt_x = _dot_exact_lhs(dt_col, expand)
    xdt = xs * dt_x
    left = lax.broadcasted_iota(jnp.int32, (lc, LANES), 1) < SSM_HEADDIM

    ys = []
    for gi in range(SSM_GROUPS):
        gs = slice(gi * GROUP_W, (gi + 1) * GROUP_W)
        bm = xact[:, HALF + gi * SSM_STATE:HALF + (gi + 1) * SSM_STATE]
        cm = xact[:, HALF + (SSM_GROUPS + gi) * SSM_STATE:HALF + (SSM_GROUPS + gi + 1) * SSM_STATE]
        cmb = cm.astype(BF16)
        cbm = _dot_nt(cmb, bm.astype(BF16))
        pair_out = []
        for pj in range(SSM_HPG // 2):
            lo = gi * GROUP_W + pj * LANES
            xpair = xdt[:, lo:lo + LANES].astype(BF16)
            yh = []
            for e in (gi * SSM_HPG + 2 * pj, gi * SSM_HPG + 2 * pj + 1):
                seg = jnp.where(tril, acs_col[:, e:e + 1] - acs_row[e:e + 1, :], NEG)
                yh.append(_dot((cbm * jnp.exp(seg)).astype(BF16), xpair))
            pair_out.append(jnp.where(left, yh[0], yh[1]))
        y_diag = jnp.concatenate(pair_out, axis=1)
        ht = ht_ref[gi]
        acs_g = acs_x[:, gs]
        y_off = _dot(cmb, ht.astype(BF16)) * jnp.exp(acs_g)
        ys.append(y_diag + y_off)
        last = acs_g[lc - 1:lc, :]
        xw = (xdt[:, gs] * jnp.exp(last - acs_g)).astype(BF16)
        ht_ref[gi] = jnp.exp(last) * ht + _dot(bm.T.astype(BF16), xw)
    y = jnp.concatenate(ys, axis=1) + dskip_ref[...] * xs
    gated = y * _silu(z_ref[...])
    ms = jnp.mean(gated * gated, axis=-1, keepdims=True)
    y_ref[...] = (gated * lax.rsqrt(ms + LN_EPS) * gain_ref[...]).astype(BF16)


def _ssd(u, dt, dtt, conv_w, conv_b, dt_bias, a_log, d_skip, ssm_norm, batch, seq):
    lc = min(SSD_CHUNK, seq)
    nc = seq // lc
    m = batch * seq
    a = -jnp.exp(a_log.astype(F32))
    pad = lambda v: jnp.pad(v, (0, LANES - SSM_HEADS))[None, :]
    full = lambda shape: pl.BlockSpec(shape, lambda b, c: (0, 0))
    return pl.pallas_call(
        functools.partial(_ssd_kernel, lc=lc),
        grid=(batch, nc),
        in_specs=[pl.BlockSpec((lc, SSM_CONV_CH), lambda b, c: (b * nc + c, 0)),
                  pl.BlockSpec((lc, HALF), lambda b, c: (b * nc + c, SSM_CONV_CH // HALF)),
                  pl.BlockSpec((lc, LANES), lambda b, c: (b * nc + c, 0)),
                  pl.BlockSpec((SSM_HEADS, lc), lambda b, c: (0, b * nc + c)),
                  full((SSM_CONV, SSM_CONV_CH)), full((1, SSM_CONV_CH)),
                  full((1, LANES)), full((SSM_HEADS, 1)), full((1, LANES)), full((SSM_HEADS, 1)),
                  full((1, HALF)), full((1, HALF))],
        out_specs=pl.BlockSpec((lc, HALF), lambda b, c: (b * nc + c, 0)),
        out_shape=jax.ShapeDtypeStruct((m, HALF), BF16),
        scratch_shapes=[pltpu.VMEM((lc + SUBLANES, SSM_CONV_CH), F32),
                        pltpu.VMEM((SSM_GROUPS, SSM_STATE, GROUP_W), F32)],
        compiler_params=_params("arbitrary", "arbitrary"),
        name="ssd",
    )(u, u, dt, dtt, conv_w, conv_b[None, :], pad(dt_bias), dt_bias[:, None], pad(a), a[:, None],
      jnp.repeat(d_skip, SSM_HEADDIM)[None, :], ssm_norm[None, :])


POOL_HALO = 16


def _pool_kernel(u_ref, w_ref, b_ref, sc_ref, p_ref, ue_ref, *, tm, tiles_per_seq):
    i = pl.program_id(0)

    @pl.when(i % tiles_per_seq == 0)
    def _():
        ue_ref[0:POOL_HALO, :] = jnp.zeros((POOL_HALO, HALF), F32)

    ue_ref[POOL_HALO:, :] = u_ref[...]
    t = (i % tiles_per_seq) * tm + lax.broadcasted_iota(jnp.int32, (tm, 1), 0)
    for gi, win in enumerate(POOL_WINDOWS):
        gs = slice(gi * POOL_GW, (gi + 1) * POOL_GW)
        wsum = ue_ref[POOL_HALO:POOL_HALO + tm, gs]
        for j in range(1, win):
            wsum = wsum + ue_ref[POOL_HALO - j:POOL_HALO - j + tm, gs]
        cnt = jnp.minimum(t + 1, win).astype(F32)
        pooled = wsum / cnt - ue_ref[POOL_HALO:POOL_HALO + tm, gs]
        y = _dot(pooled.astype(BF16), w_ref[gi]) + b_ref[:, gs]
        p_ref[:, gs] = (y * sc_ref[:, gs]).astype(BF16)
    ue_ref[0:POOL_HALO, :] = ue_ref[tm:tm + POOL_HALO, :]


def _pool(u, pool_w, pool_b, pool_scale, seq):
    m = u.shape[0]
    tm = min(ROW_TILE, seq)
    full2 = lambda shape: pl.BlockSpec(shape, lambda i: (0, 0))
    return pl.pallas_call(
        functools.partial(_pool_kernel, tm=tm, tiles_per_seq=seq // tm),
        grid=(m // tm,),
        in_specs=[pl.BlockSpec((tm, HALF), lambda i: (i, (SSM_CONV_CH + HALF) // HALF)),
                  pl.BlockSpec((len(POOL_WINDOWS), POOL_GW, POOL_GW), lambda i: (0, 0, 0)),
                  full2((1, HALF)), full2((1, HALF))],
        out_specs=pl.BlockSpec((tm, HALF), lambda i: (i, 0)),
        out_shape=jax.ShapeDtypeStruct((m, HALF), BF16),
        scratch_shapes=[pltpu.VMEM((tm + POOL_HALO, HALF), F32)],
        compiler_params=_params("arbitrary"),
        name="pool",
    )(u, pool_w.astype(BF16), pool_b[None, :], pool_scale[None, :])


def kernel(x, ev_w_in, ev_b_in, ev_ml_norm, ev_w_out, od_w_in, od_conv_w, od_conv_b, od_dt_bias, od_a_log,
           od_d_skip, od_ssm_norm, od_pool_w, od_pool_b, od_pool_scale, od_w_out, ffn_w_up, ffn_conv_w,
           ffn_conv_b, ffn_w_down, ln1_g, ln1_b, ln2_g, ln2_b):
    batch, seq, _ = x.shape
    h = x.reshape(batch * seq, D_MODEL)
    for layer in range(DEPTH):
        j = layer // 2
        if layer % 2 == 0:
            qkv, mo, g, gt, vt = _inproj_even(h, ev_w_in[j], ev_b_in[j])
            a, kext = _mlstm(qkv, mo, g, gt, ev_ml_norm[j][None, :], batch, seq)
            b = _fox(qkv, kext, vt, batch, seq)
            w_out = ev_w_out[j]
        else:
            u, dt, dtt = _inproj_odd(h, od_w_in[j])
            a = _ssd(u, dt, dtt, od_conv_w[j], od_conv_b[j], od_dt_bias[j], od_a_log[j], od_d_skip[j],
                     od_ssm_norm[j], batch, seq)
            b = _pool(u, od_pool_w[j], od_pool_b[j], od_pool_scale[j], seq)
            w_out = od_w_out[j]
        h = _outproj_ln(a, b, h, w_out, ln1_g[layer], ln1_b[layer])
        h = _ffn(h, ffn_w_up[layer], ffn_conv_w[layer], ffn_conv_b[layer], ffn_w_down[layer],
                 ln2_g[layer], ln2_b[layer], seq)
    return h.reshape(batch, seq, D_MODEL)
```

```python
import functools
import math

import jax
import jax.numpy as jnp
import numpy as np
from jax import lax
from jax.experimental import pallas as pl
from jax.experimental.pallas import tpu as pltpu

F32 = jnp.float32
BF16 = jnp.bfloat16

D_MODEL = 1024
DEPTH = 4
HALF = D_MODEL // 2
ML_HEADS = 4
ML_HD = HALF // ML_HEADS
FOX_HEADS = 8
FOX_HD = HALF // FOX_HEADS
SSM_HEADDIM = 64
SSM_HEADS = HALF // SSM_HEADDIM
SSM_GROUPS = 2
SSM_STATE = 128
SSM_CONV = 4
SSM_CONV_CH = HALF + 2 * SSM_GROUPS * SSM_STATE
POOL_WINDOWS = (2, 4, 8, 16)
POOL_GW = HALF // len(POOL_WINDOWS)
D_FF = 256 * ((8 * D_MODEL // 3 + 255) // 256)
FFN_CONV = 3
LN_EPS = 1e-5
ALPHA = (2.0 * DEPTH) ** 0.25

LANES = 128
SUBLANES = 8
NEG = -1e30
LOG2E = 1.4426950408889634
VMEM_LIMIT = 56 * 1024 * 1024

ROW_TILE = 512
ML_CHUNK = 256
SSD_CHUNK = 256
FOX_BLOCK = 512
FOX_UNROLL = 4
FFN_COLS = 256


def _params(*sem):
    return pltpu.CompilerParams(dimension_semantics=sem, vmem_limit_bytes=VMEM_LIMIT)


def _dot(a, b):
    return jnp.dot(a, b, preferred_element_type=F32)


def _dot_nt(a, b):
    return lax.dot_general(a, b, (((1,), (1,)), ((), ())), preferred_element_type=F32)


def _split3(x):
    h1 = x.astype(BF16)
    r1 = x - h1.astype(F32)
    h2 = r1.astype(BF16)
    h3 = (r1 - h2.astype(F32)).astype(BF16)
    return h1, h2, h3


def _dot_exact_rhs(t, x):
    h1, h2, h3 = _split3(x)
    return _dot(t, h1) + _dot(t, h2) + _dot(t, h3)


def _dot_exact_lhs(x, t):
    h1, h2, h3 = _split3(x)
    return _dot(h1, t) + _dot(h2, t) + _dot(h3, t)


def _log_sigmoid(x):
    return jnp.minimum(x, 0.0) - jnp.log(1.0 + jnp.exp(-jnp.abs(x)))


def _softplus(x):
    return jnp.maximum(x, 0.0) + jnp.log(1.0 + jnp.exp(-jnp.abs(x)))


def _sigmoid(x):
    return 1.0 / (1.0 + jnp.exp(-x))


def _silu(x):
    return x * _sigmoid(x)


def _tri(n, lower):
    r = lax.broadcasted_iota(jnp.int32, (n, n), 0)
    c = lax.broadcasted_iota(jnp.int32, (n, n), 1)
    return (c <= r) if lower else (r <= c)


def _layer_norm(v, g, b):
    mu = jnp.mean(v, axis=-1, keepdims=True)
    d = v - mu
    var = jnp.mean(d * d, axis=-1, keepdims=True)
    return d * lax.rsqrt(var + LN_EPS) * g + b


EV_MAIN = 4 * HALF
EV_GATES = 2 * ML_HEADS + FOX_HEADS
FOX_QSCALE = FOX_HD ** -0.5 * LOG2E


def _inproj_even_kernel(x_ref, wm_ref, bm_ref, wo_ref, bo_ref, wg_ref, bg_ref, wgt_ref, bgt_ref,
                        wqvt_ref, bqvt_ref, qkv_ref, mo_ref, g_ref, gt_ref, qvt_ref):
    xb = x_ref[...].astype(BF16)
    for j in range(EV_MAIN // HALF):
        sl = slice(j * HALF, (j + 1) * HALF)
        qkv_ref[:, sl] = (_dot(xb, wm_ref[:, sl]) + bm_ref[:, sl]).astype(BF16)
    mo_ref[...] = _dot(xb, wo_ref[...]) + bo_ref[...]
    g_ref[...] = _dot(xb, wg_ref[...]) + bg_ref[...]
    gt_ref[...] = _dot_nt(wgt_ref[...], xb) + bgt_ref[...]
    fqt = (_dot_nt(wqvt_ref[0:HALF, :], xb) + bqvt_ref[0:HALF, :]) * FOX_QSCALE
    qvt_ref[0:HALF, :] = fqt.astype(BF16)
    qvt_ref[HALF:, :] = (_dot_nt(wqvt_ref[HALF:, :], xb) + bqvt_ref[HALF:, :]).astype(BF16)


def _inproj_even(x, w_in, b_in):
    m = x.shape[0]
    tm = ROW_TILE
    o = [0, HALF, 2 * HALF, 3 * HALF, 4 * HALF, 4 * HALF + ML_HEADS, 4 * HALF + 2 * ML_HEADS]
    fq0 = o[6]
    fk0 = fq0 + HALF
    fv0 = fq0 + 2 * HALF
    ff0 = fq0 + 3 * HALF
    wm = jnp.concatenate([w_in[:, :3 * HALF], w_in[:, fk0:fv0]], axis=1).astype(BF16)
    bm = jnp.concatenate([b_in[:3 * HALF], b_in[fk0:fv0]])[None, :]
    wvt = jnp.concatenate([w_in[:, fq0:fk0], w_in[:, fv0:ff0]], axis=1).T.astype(BF16)
    bvt = jnp.concatenate([b_in[fq0:fk0], b_in[fv0:ff0]])[:, None]
    wo = w_in[:, 3 * HALF:4 * HALF].astype(BF16)
    bo = b_in[3 * HALF:4 * HALF][None, :]
    wg16 = jnp.concatenate([w_in[:, o[4]:o[6]], w_in[:, ff0:]], axis=1)
    bg16 = jnp.concatenate([b_in[o[4]:o[6]], b_in[ff0:]])
    wg = jnp.pad(wg16, ((0, 0), (0, LANES - EV_GATES))).astype(BF16)
    bg = jnp.pad(bg16, (0, LANES - EV_GATES))[None, :]
    wgt = wg16.T.astype(BF16)
    bgt = bg16[:, None]
    full = lambda shape: pl.BlockSpec(shape, lambda i: (0, 0))
    return pl.pallas_call(
        _inproj_even_kernel,
        grid=(m // tm,),
        in_specs=[pl.BlockSpec((tm, D_MODEL), lambda i: (i, 0)),
                  full((D_MODEL, EV_MAIN)), full((1, EV_MAIN)),
                  full((D_MODEL, HALF)), full((1, HALF)),
                  full((D_MODEL, LANES)), full((1, LANES)),
                  full((EV_GATES, D_MODEL)), full((EV_GATES, 1)),
                  full((2 * HALF, D_MODEL)), full((2 * HALF, 1))],
        out_specs=[pl.BlockSpec((tm, EV_MAIN), lambda i: (i, 0)),
                   pl.BlockSpec((tm, HALF), lambda i: (i, 0)),
                   pl.BlockSpec((tm, LANES), lambda i: (i, 0)),
                   pl.BlockSpec((EV_GATES, tm), lambda i: (0, i)),
                   pl.BlockSpec((2 * HALF, tm), lambda i: (0, i))],
        out_shape=[jax.ShapeDtypeStruct((m, EV_MAIN), BF16),
                   jax.ShapeDtypeStruct((m, HALF), F32),
                   jax.ShapeDtypeStruct((m, LANES), F32),
                   jax.ShapeDtypeStruct((EV_GATES, m), F32),
                   jax.ShapeDtypeStruct((2 * HALF, m), BF16)],
        compiler_params=_params("arbitrary"),
        name="inproj_even",
    )(x, wm, bm, wo, bo, wg, bg, wgt, bgt, wvt, bvt)


ML_LOG_SCALE = math.log(ML_HD ** -0.5)


FOX_PAIRS = FOX_HEADS // 2
KEXT_W = 2 * LANES
BIAS_PARTS = 3


def _bias_routing():
    r = np.zeros((BIAS_PARTS, LANES, FOX_PAIRS * LANES), np.float32)
    for head in range(FOX_HEADS):
        for part in range(BIAS_PARTS):
            r[part, 2 * ML_HEADS + head, (head // 2) * LANES + BIAS_PARTS * (head % 2) + part] = 1.0
    return jnp.asarray(r, BF16)


def _mlstm_kernel(q_ref, k_ref, v_ref, fk_ref, mo_ref, g_ref, gt_ref, gain_ref, route_ref, h_ref, kext_ref,
                  ct_ref, m_ref, fcar_ref, *, lc):
    c = pl.program_id(1)

    @pl.when(c == 0)
    def _():
        ct_ref[...] = jnp.zeros_like(ct_ref)
        m_ref[...] = jnp.zeros_like(m_ref)
        fcar_ref[...] = jnp.zeros_like(fcar_ref)

    tril = _tri(lc, True)
    tril_b = jnp.where(tril, 1.0, 0.0).astype(BF16)
    triu_b = jnp.where(_tri(lc, False), 1.0, 0.0).astype(BF16)
    g = g_ref[...]
    gt = gt_ref[...]
    bc_col = _dot_exact_rhs(tril_b, _log_sigmoid(g))
    bc_row = _dot_exact_lhs(_log_sigmoid(gt), triu_b)

    f_blk = bc_col + fcar_ref[0:1, :]
    fcar_ref[...] = jnp.broadcast_to(f_blk[lc - 1:lc, :], fcar_ref.shape)
    parts = _split3(f_blk * LOG2E)
    bias = sum(_dot(parts[i], route_ref[i]) for i in range(BIAS_PARTS)).astype(BF16)
    for p in range(FOX_PAIRS):
        kext_ref[:, p * KEXT_W:p * KEXT_W + LANES] = fk_ref[:, p * LANES:(p + 1) * LANES]
        kext_ref[:, p * KEXT_W + LANES:(p + 1) * KEXT_W] = bias[:, p * LANES:(p + 1) * LANES]

    lane = lax.broadcasted_iota(jnp.int32, (lc, ML_HD), 1)
    ones_col = jnp.where(lane == 0, 1.0, 0.0).astype(BF16)
    heads = range(ML_HEADS)
    hs = [slice(h * ML_HD, (h + 1) * ML_HD) for h in heads]
    vext = [jnp.concatenate([v_ref[:, hs[h]], ones_col], axis=1) for h in heads]
    bcl = [bc_col[:, ML_HEADS + h:ML_HEADS + h + 1] for h in heads]
    bcr = [bc_row[ML_HEADS + h:ML_HEADS + h + 1, :] for h in heads]
    m_st = [m_ref[h:h + 1, 0:1] for h in heads]
    ct = [ct_ref[h] for h in heads]

    s = [_dot_nt(q_ref[:, hs[h]], k_ref[:, hs[h]]) for h in heads]
    inter = [_dot(q_ref[:, hs[h]], ct[h].astype(BF16)) for h in heads]

    for h in heads:
        iil = g[:, h:h + 1]
        iir = gt[h:h + 1, :]
        btot = bcr[h][:, lc - 1:lc]
        m_new = jnp.maximum(btot + m_st[h], jnp.max(btot - bcr[h] + iir, axis=1, keepdims=True))
        decay = jnp.exp(btot + m_st[h] - m_new)
        ws = jnp.exp(btot - bcl[h] + iil - m_new)
        kwt = (k_ref[:, hs[h]].astype(F32) * ws).T.astype(BF16)
        ct_ref[h] = decay * ct[h] + _dot(kwt, vext[h])
        m_ref[h:h + 1, :] = jnp.broadcast_to(m_new, (1, LANES))

    mt, w_inter, intra = [], [], []
    for h in heads:
        iir = gt[h:h + 1, :]
        a_col = bcl[h] + m_st[h]
        dmat = jnp.where(tril, bcl[h] + (iir - bcr[h]), NEG)
        mt.append(jnp.maximum(a_col, jnp.max(dmat, axis=1, keepdims=True)))
        mts = mt[h] - ML_LOG_SCALE
        w_inter.append(jnp.exp(a_col - mts))
        pw = (s[h] * jnp.exp(dmat - mts)).astype(BF16)
        intra.append(_dot(pw, vext[h]))

    for h in heads:
        tot = intra[h] + w_inter[h] * inter[h]
        num = tot[:, :ML_HD]
        den = tot[:, ML_HD:ML_HD + 1]
        hh = num / jnp.maximum(jnp.abs(den), jnp.exp(-mt[h]))
        mu = jnp.mean(hh, axis=-1, keepdims=True)
        d = hh - mu
        var = jnp.mean(d * d, axis=-1, keepdims=True)
        hn = d * lax.rsqrt(var + LN_EPS) * gain_ref[:, hs[h]]
        h_ref[:, hs[h]] = (_sigmoid(mo_ref[:, hs[h]]) * hn).astype(BF16)


def _mlstm(qkv, mo, g, gt, gain, batch, seq):
    lc = min(ML_CHUNK, seq)
    nc = seq // lc
    m = batch * seq
    row = lambda j: pl.BlockSpec((lc, HALF), lambda b, c, j=j: (b * nc + c, j))
    return pl.pallas_call(
        functools.partial(_mlstm_kernel, lc=lc),
        grid=(batch, nc),
        in_specs=[row(0), row(1), row(2), row(3),
                  pl.BlockSpec((lc, HALF), lambda b, c: (b * nc + c, 0)),
                  pl.BlockSpec((lc, LANES), lambda b, c: (b * nc + c, 0)),
                  pl.BlockSpec((EV_GATES, lc), lambda b, c: (0, b * nc + c)),
                  pl.BlockSpec((1, HALF), lambda b, c: (0, 0)),
                  pl.BlockSpec((BIAS_PARTS, LANES, FOX_PAIRS * LANES), lambda b, c: (0, 0, 0))],
        out_specs=[pl.BlockSpec((lc, HALF), lambda b, c: (b * nc + c, 0)),
                   pl.BlockSpec((lc, FOX_PAIRS * KEXT_W), lambda b, c: (b * nc + c, 0))],
        out_shape=[jax.ShapeDtypeStruct((m, HALF), BF16),
                   jax.ShapeDtypeStruct((m, FOX_PAIRS * KEXT_W), BF16)],
        scratch_shapes=[pltpu.VMEM((ML_HEADS, ML_HD, 2 * ML_HD), F32),
                        pltpu.VMEM((SUBLANES, LANES), F32),
                        pltpu.VMEM((SUBLANES, LANES), F32)],
        compiler_params=_params("arbitrary", "arbitrary"),
        name="mlstm",
    )(qkv, qkv, qkv, qkv, mo, g, gt, gain, _bias_routing())


BF16_ROWS = 2 * SUBLANES
FOX_ACC_ROWS = FOX_HD + BF16_ROWS


def _fox_kernel(qt_ref, kext_ref, vt_ref, o_ref, m_sc, acc_sc, s0_sc, s1_sc, mx0_sc, mx1_sc, *, blk):
    qi = pl.program_id(2)
    q2 = qt_ref[...]
    row = lax.broadcasted_iota(jnp.int32, (LANES, blk), 0)
    top = row < FOX_HD
    zero = jnp.zeros_like(q2)
    qe = []
    for hd in range(2):
        pick = (row >= BIAS_PARTS * hd) & (row < BIAS_PARTS * (hd + 1))
        minus1 = jnp.where(pick, -1.0, 0.0).astype(BF16)
        qh = jnp.where(top, q2, zero) if hd == 0 else jnp.where(top, zero, q2)
        qe.append(jnp.concatenate([qh, minus1], axis=0))

    m_sc[...] = jnp.full_like(m_sc, NEG)
    acc_sc[...] = jnp.zeros_like(acc_sc)
    s_bufs = (s0_sc, s1_sc)
    mx_bufs = (mx0_sc, mx1_sc)
    ones_rows = jnp.where(lax.broadcasted_iota(jnp.int32, (BF16_ROWS, blk), 0) == 0, 1.0, 0.0).astype(BF16)

    def scores(kj, slot, masked):
        k0 = pl.multiple_of(kj * blk, blk)
        ke = kext_ref[pl.ds(k0, blk), :]
        for hd in range(2):
            s = _dot(ke, qe[hd])
            if masked:
                s = jnp.where(_tri(blk, False), s, NEG)
            s_bufs[slot][hd] = s
            mx_bufs[slot][hd] = jnp.max(s, axis=0, keepdims=True)

    def accumulate(kj, slot):
        k0 = pl.multiple_of(kj * blk, blk)
        for hd in range(2):
            rows = slice(hd * FOX_HD, (hd + 1) * FOX_HD)
            m_prev = m_sc[hd]
            m_next = jnp.maximum(m_prev, mx_bufs[slot][hd])
            alpha = jnp.exp2(m_prev - m_next)
            p = jnp.exp2(s_bufs[slot][hd] - m_next).astype(BF16)
            m_sc[hd] = m_next
            vt1 = jnp.concatenate([vt_ref[rows, pl.ds(k0, blk)], ones_rows], axis=0)
            acc_sc[hd] = alpha * acc_sc[hd] + _dot(vt1, p)

    def key_block(t):
        return jnp.where(t == 0, qi, t - 1)

    def step(t, slot):
        scores(t - 1, slot, False)
        accumulate(key_block(t - 1), 1 - slot)

    scores(qi, 0, True)

    def body(i, carry):
        for u in range(FOX_UNROLL):
            step(FOX_UNROLL * i + u + 1, (u + 1) % 2)
        return carry

    lax.fori_loop(0, qi // FOX_UNROLL, body, 0)
    done = (qi // FOX_UNROLL) * FOX_UNROLL
    width = FOX_UNROLL // 2
    while width >= 1:
        @pl.when(qi & width != 0)
        def _(done=done, width=width):
            for u in range(width):
                step(done + u + 1, (u + 1) % 2)
        done = done + (qi & width)
        width //= 2

    @pl.when(qi % 2 == 1)
    def _():
        accumulate(key_block(qi), 1)

    @pl.when(qi % 2 == 0)
    def _():
        accumulate(key_block(qi), 0)

    out = [acc_sc[hd, 0:FOX_HD, :] * (1.0 / acc_sc[hd, FOX_HD:FOX_HD + 1, :]) for hd in range(2)]
    o_ref[...] = jnp.concatenate(out, axis=0).T.astype(BF16)


def _fox(qvt, kext, batch, seq):
    blk = min(FOX_BLOCK, seq)
    nq = seq // blk
    m = batch * seq
    return pl.pallas_call(
        functools.partial(_fox_kernel, blk=blk),
        grid=(batch, FOX_PAIRS, nq),
        in_specs=[pl.BlockSpec((LANES, blk), lambda b, p, i: (p, b * nq + i)),
                  pl.BlockSpec((seq, KEXT_W), lambda b, p, i: (b, p)),
                  pl.BlockSpec((LANES, seq), lambda b, p, i: (FOX_PAIRS + p, b))],
        out_specs=pl.BlockSpec((blk, LANES), lambda b, p, i: (b * nq + i, p)),
        out_shape=jax.ShapeDtypeStruct((m, HALF), BF16),
        scratch_shapes=[pltpu.VMEM((2, 1, blk), F32),
                        pltpu.VMEM((2, FOX_ACC_ROWS, blk), F32),
                        pltpu.VMEM((2, blk, blk), F32),
                        pltpu.VMEM((2, blk, blk), F32),
                        pltpu.VMEM((2, 1, blk), F32),
                        pltpu.VMEM((2, 1, blk), F32)],
        compiler_params=_params("arbitrary", "arbitrary", "arbitrary"),
        name="fox_attention",
    )(qvt, kext, qvt)


def _outproj_kernel(a_ref, b_ref, x_ref, w_ref, g_ref, beta_ref, o_ref):
    ab = jnp.concatenate([a_ref[...], b_ref[...]], axis=1)
    y = _dot(ab, w_ref[...])
    o_ref[...] = _layer_norm(ALPHA * x_ref[...] + y, g_ref[...], beta_ref[...])


def _outproj_ln(a, b, x, w_out, ln_g, ln_b):
    m = x.shape[0]
    tm = ROW_TILE
    row = lambda width: pl.BlockSpec((tm, width), lambda i: (i, 0))
    full = lambda shape: pl.BlockSpec(shape, lambda i: (0, 0))
    return pl.pallas_call(
        _outproj_kernel,
        grid=(m // tm,),
        in_specs=[row(HALF), row(HALF), row(D_MODEL), full((D_MODEL, D_MODEL)),
                  full((1, D_MODEL)), full((1, D_MODEL))],
        out_specs=row(D_MODEL),
        out_shape=jax.ShapeDtypeStruct((m, D_MODEL), F32),
        compiler_params=_params("arbitrary"),
        name="outproj_ln",
    )(a, b, x, w_out.astype(BF16), ln_g[None, :], ln_b[None, :])


def _ffn_kernel(x_ref, wu_ref, cw_ref, cb_ref, wd_ref, g_ref, beta_ref, o_ref,
                he_ref, hprev_ref, act_ref, *, tm, tiles_per_seq):
    i = pl.program_id(0)

    @pl.when(i % tiles_per_seq == 0)
    def _():
        hprev_ref[...] = jnp.zeros_like(hprev_ref)

    xb = x_ref[...].astype(BF16)
    cw = FFN_COLS

    def conv_cols(cs):
        he_ref[0:SUBLANES, :] = hprev_ref[:, cs]
        he_ref[SUBLANES:, :] = _dot(xb, wu_ref[:, cs])
        hprev_ref[:, cs] = he_ref[tm:tm + SUBLANES, :]
        he = he_ref[...]
        conv = cb_ref[:, cs] + cw_ref[FFN_CONV - 1:FFN_CONV, cs] * he[SUBLANES:, :]
        for back in range(1, FFN_CONV):
            conv = conv + cw_ref[FFN_CONV - 1 - back:FFN_CONV - back, cs] * pltpu.roll(he, back, 0)[SUBLANES:, :]
        return conv

    for j in range(D_FF // cw):
        val = conv_cols(slice(j * cw, (j + 1) * cw))
        gate = conv_cols(slice(D_FF + j * cw, D_FF + (j + 1) * cw))
        act_ref[:, j * cw:(j + 1) * cw] = (_silu(gate) * val).astype(BF16)
    f = _dot(act_ref[...], wd_ref[...])
    o_ref[...] = _layer_norm(ALPHA * x_ref[...] + f, g_ref[...], beta_ref[...])


def _ffn(x, w_up, conv_w, conv_b, w_down, ln_g, ln_b, seq):
    m = x.shape[0]
    tm = min(ROW_TILE, seq)
    full = lambda shape: pl.BlockSpec(shape, lambda i: (0, 0))
    return pl.pallas_call(
        functools.partial(_ffn_kernel, tm=tm, tiles_per_seq=seq // tm),
        grid=(m // tm,),
        in_specs=[pl.BlockSpec((tm, D_MODEL), lambda i: (i, 0)),
                  full((D_MODEL, 2 * D_FF)), full((FFN_CONV, 2 * D_FF)), full((1, 2 * D_FF)),
                  full((D_FF, D_MODEL)), full((1, D_MODEL)), full((1, D_MODEL))],
        out_specs=pl.BlockSpec((tm, D_MODEL), lambda i: (i, 0)),
        out_shape=jax.ShapeDtypeStruct((m, D_MODEL), F32),
        scratch_shapes=[pltpu.VMEM((tm + SUBLANES, FFN_COLS), F32),
                        pltpu.VMEM((SUBLANES, 2 * D_FF), F32),
                        pltpu.VMEM((tm, D_FF), BF16)],
        compiler_params=_params("arbitrary"),
        name="conv_ffn",
    )(x, w_up.astype(BF16), conv_w, conv_b[None, :], w_down.astype(BF16), ln_g[None, :], ln_b[None, :])


OD_MAIN = SSM_CONV_CH + 2 * HALF


def _inproj_odd_kernel(x_ref, wm_ref, wd_ref, wdt_ref, u_ref, dt_ref, dtt_ref):
    xb = x_ref[...].astype(BF16)
    for j in range(OD_MAIN // HALF):
        sl = slice(j * HALF, (j + 1) * HALF)
        u_ref[:, sl] = _dot(xb, wm_ref[:, sl])
    dt_ref[...] = _dot(xb, wd_ref[...])
    dtt_ref[...] = _dot_nt(wdt_ref[...], xb)


def _inproj_odd(x, w_in):
    m = x.shape[0]
    tm = ROW_TILE
    z1 = HALF
    x1 = z1 + SSM_CONV_CH
    d1 = x1 + SSM_HEADS
    wm = jnp.concatenate([w_in[:, z1:x1], w_in[:, :z1], w_in[:, d1:]], axis=1).astype(BF16)
    wd8 = w_in[:, x1:d1]
    wd = jnp.pad(wd8, ((0, 0), (0, LANES - SSM_HEADS))).astype(BF16)
    wdt = wd8.T.astype(BF16)
    full = lambda shape: pl.BlockSpec(shape, lambda i: (0, 0))
    return pl.pallas_call(
        _inproj_odd_kernel,
        grid=(m // tm,),
        in_specs=[pl.BlockSpec((tm, D_MODEL), lambda i: (i, 0)),
                  full((D_MODEL, OD_MAIN)), full((D_MODEL, LANES)), full((SSM_HEADS, D_MODEL))],
        out_specs=[pl.BlockSpec((tm, OD_MAIN), lambda i: (i, 0)),
                   pl.BlockSpec((tm, LANES), lambda i: (i, 0)),
                   pl.BlockSpec((SSM_HEADS, tm), lambda i: (0, i))],
        out_shape=[jax.ShapeDtypeStruct((m, OD_MAIN), F32),
                   jax.ShapeDtypeStruct((m, LANES), F32),
                   jax.ShapeDtypeStruct((SSM_HEADS, m), F32)],
        compiler_params=_params("arbitrary"),
        name="inproj_odd",
    )(x, wm, wd, wdt)


SSM_HPG = SSM_HEADS // SSM_GROUPS
GROUP_W = SSM_HPG * SSM_HEADDIM


def _ssd_kernel(xbc_ref, z_ref, dt_ref, dtt_ref, cw_ref, cb_ref, dtb_ref, dtbt_ref, a_ref, at_ref,
                dskip_ref, gain_ref, y_ref, xe_ref, ht_ref, *, lc):
    c = pl.program_id(1)

    @pl.when(c == 0)
    def _():
        xe_ref[0:SUBLANES, :] = jnp.zeros((SUBLANES, SSM_CONV_CH), F32)
        ht_ref[...] = jnp.zeros_like(ht_ref)

    xe_ref[SUBLANES:, :] = xbc_ref[...]
    xe = xe_ref[...]
    conv = cb_ref[...] + cw_ref[SSM_CONV - 1:SSM_CONV, :] * xe[SUBLANES:, :]
    for back in range(1, SSM_CONV):
        conv = conv + cw_ref[SSM_CONV - 1 - back:SSM_CONV - back, :] * pltpu.roll(xe, back, 0)[SUBLANES:, :]
    xe_ref[0:SUBLANES, :] = xe_ref[lc:lc + SUBLANES, :]
    xact = _silu(conv)
    xs = xact[:, :HALF]

    tril = _tri(lc, True)
    tril_b = jnp.where(tril, 1.0, 0.0).astype(BF16)
    triu_b = jnp.where(_tri(lc, False), 1.0, 0.0).astype(BF16)
    er = lax.broadcasted_iota(jnp.int32, (LANES, HALF), 0)
    ec = lax.broadcasted_iota(jnp.int32, (LANES, HALF), 1)
    expand = jnp.where(jnp.right_shift(ec, 6) == er, 1.0, 0.0).astype(BF16)

    dt_col = _softplus(dt_ref[...] + dtb_ref[...])
    dt_row = _softplus(dtt_ref[...] + dtbt_ref[...])
    acs_col = _dot_exact_rhs(tril_b, dt_col * a_ref[...])
    acs_row = _dot_exact_lhs(dt_row * at_ref[...], triu_b)
    acs_x = _dot_exact_lhs(acs_col, expand)
    dt_x = _dot_exact_lhs(dt_col, expand)
    xdt = xs * dt_x
    left = lax.broadcasted_iota(jnp.int32, (lc, LANES), 1) < SSM_HEADDIM

    groups = range(SSM_GROUPS)
    gs = [slice(gi * GROUP_W, (gi + 1) * GROUP_W) for gi in groups]
    bm = [xact[:, HALF + gi * SSM_STATE:HALF + (gi + 1) * SSM_STATE] for gi in groups]
    cmb = [xact[:, HALF + (SSM_GROUPS + gi) * SSM_STATE:HALF + (SSM_GROUPS + gi + 1) * SSM_STATE].astype(BF16)
           for gi in groups]
    ht = [ht_ref[gi] for gi in groups]
    cbm = [_dot_nt(cmb[gi], bm[gi].astype(BF16)) for gi in groups]
    off = [_dot(cmb[gi], ht[gi].astype(BF16)) for gi in groups]
    for gi in groups:
        acs_g = acs_x[:, gs[gi]]
        last = acs_g[lc - 1:lc, :]
        xw = (xdt[:, gs[gi]] * jnp.exp(last - acs_g)).astype(BF16)
        ht_ref[gi] = jnp.exp(last) * ht[gi] + _dot(bm[gi].T.astype(BF16), xw)
    ys = []
    for gi in groups:
        pair_out = []
        for pj in range(SSM_HPG // 2):
            lo = gi * GROUP_W + pj * LANES
            xpair = xdt[:, lo:lo + LANES].astype(BF16)
            yh = []
            for e in (gi * SSM_HPG + 2 * pj, gi * SSM_HPG + 2 * pj + 1):
                seg = jnp.where(tril, acs_col[:, e:e + 1] - acs_row[e:e + 1, :], NEG)
                yh.append(_dot((cbm[gi] * jnp.exp(seg)).astype(BF16), xpair))
            pair_out.append(jnp.where(left, yh[0], yh[1]))
        y_diag = jnp.concatenate(pair_out, axis=1)
        ys.append(y_diag + off[gi] * jnp.exp(acs_x[:, gs[gi]]))
    y = jnp.concatenate(ys, axis=1) + dskip_ref[...] * xs
    gated = y * _silu(z_ref[...])
    ms = jnp.mean(gated * gated, axis=-1, keepdims=True)
    y_ref[...] = (gated * lax.rsqrt(ms + LN_EPS) * gain_ref[...]).astype(BF16)


def _ssd(u, dt, dtt, conv_w, conv_b, dt_bias, a_log, d_skip, ssm_norm, batch, seq):
    lc = min(SSD_CHUNK, seq)
    nc = seq // lc
    m = batch * seq
    a = -jnp.exp(a_log.astype(F32))
    pad = lambda v: jnp.pad(v, (0, LANES - SSM_HEADS))[None, :]
    full = lambda shape: pl.BlockSpec(shape, lambda b, c: (0, 0))
    return pl.pallas_call(
        functools.partial(_ssd_kernel, lc=lc),
        grid=(batch, nc),
        in_specs=[pl.BlockSpec((lc, SSM_CONV_CH), lambda b, c: (b * nc + c, 0)),
                  pl.BlockSpec((lc, HALF), lambda b, c: (b * nc + c, SSM_CONV_CH // HALF)),
                  pl.BlockSpec((lc, LANES), lambda b, c: (b * nc + c, 0)),
                  pl.BlockSpec((SSM_HEADS, lc), lambda b, c: (0, b * nc + c)),
                  full((SSM_CONV, SSM_CONV_CH)), full((1, SSM_CONV_CH)),
                  full((1, LANES)), full((SSM_HEADS, 1)), full((1, LANES)), full((SSM_HEADS, 1)),
                  full((1, HALF)), full((1, HALF))],
        out_specs=pl.BlockSpec((lc, HALF), lambda b, c: (b * nc + c, 0)),
        out_shape=jax.ShapeDtypeStruct((m, HALF), BF16),
        scratch_shapes=[pltpu.VMEM((lc + SUBLANES, SSM_CONV_CH), F32),
                        pltpu.VMEM((SSM_GROUPS, SSM_STATE, GROUP_W), F32)],
        compiler_params=_params("arbitrary", "arbitrary"),
        name="ssd",
    )(u, u, dt, dtt, conv_w, conv_b[None, :], pad(dt_bias), dt_bias[:, None], pad(a), a[:, None],
      jnp.repeat(d_skip, SSM_HEADDIM)[None, :], ssm_norm[None, :])


POOL_HALO = 16


def _pool_kernel(u_ref, w_ref, b_ref, sc_ref, p_ref, ue_ref, *, tm, tiles_per_seq):
    i = pl.program_id(0)

    @pl.when(i % tiles_per_seq == 0)
    def _():
        ue_ref[0:POOL_HALO, :] = jnp.zeros((POOL_HALO, HALF), F32)

    ue_ref[POOL_HALO:, :] = u_ref[...]
    t = (i % tiles_per_seq) * tm + lax.broadcasted_iota(jnp.int32, (tm, 1), 0)
    for gi, win in enumerate(POOL_WINDOWS):
        gs = slice(gi * POOL_GW, (gi + 1) * POOL_GW)
        acc = ue_ref[:, gs]
        span = 1
        while span < win:
            acc = acc + pltpu.roll(acc, span, 0)
            span *= 2
        cnt = jnp.minimum(t + 1, win).astype(F32)
        pooled = acc[POOL_HALO:, :] / cnt - ue_ref[POOL_HALO:POOL_HALO + tm, gs]
        y = _dot(pooled.astype(BF16), w_ref[gi]) + b_ref[:, gs]
        p_ref[:, gs] = (y * sc_ref[:, gs]).astype(BF16)
    ue_ref[0:POOL_HALO, :] = ue_ref[tm:tm + POOL_HALO, :]


def _pool(u, pool_w, pool_b, pool_scale, seq):
    m = u.shape[0]
    tm = min(ROW_TILE, seq)
    full2 = lambda shape: pl.BlockSpec(shape, lambda i: (0, 0))
    return pl.pallas_call(
        functools.partial(_pool_kernel, tm=tm, tiles_per_seq=seq // tm),
        grid=(m // tm,),
        in_specs=[pl.BlockSpec((tm, HALF), lambda i: (i, (SSM_CONV_CH + HALF) // HALF)),
                  pl.BlockSpec((len(POOL_WINDOWS), POOL_GW, POOL_GW), lambda i: (0, 0, 0)),
                  full2((1, HALF)), full2((1, HALF))],
        out_specs=pl.BlockSpec((tm, HALF), lambda i: (i, 0)),
        out_shape=jax.ShapeDtypeStruct((m, HALF), BF16),
        scratch_shapes=[pltpu.VMEM((tm + POOL_HALO, HALF), F32)],
        compiler_params=_params("arbitrary"),
        name="pool",
    )(u, pool_w.astype(BF16), pool_b[None, :], pool_scale[None, :])


def kernel(x, ev_w_in, ev_b_in, ev_ml_norm, ev_w_out, od_w_in, od_conv_w, od_conv_b, od_dt_bias, od_a_log,
           od_d_skip, od_ssm_norm, od_pool_w, od_pool_b, od_pool_scale, od_w_out, ffn_w_up, ffn_conv_w,
           ffn_conv_b, ffn_w_down, ln1_g, ln1_b, ln2_g, ln2_b):
    batch, seq, _ = x.shape
    h = x.reshape(batch * seq, D_MODEL)
    for layer in range(DEPTH):
        j = layer // 2
        if layer % 2 == 0:
            qkv, mo, g, gt, qvt = _inproj_even(h, ev_w_in[j], ev_b_in[j])
            a, kext = _mlstm(qkv, mo, g, gt, ev_ml_norm[j][None, :], batch, seq)
            b = _fox(qvt, kext, batch, seq)
            w_out = ev_w_out[j]
        else:
            u, dt, dtt = _inproj_odd(h, od_w_in[j])
            a = _ssd(u, dt, dtt, od_conv_w[j], od_conv_b[j], od_dt_bias[j], od_a_log[j], od_d_skip[j],
                     od_ssm_norm[j], batch, seq)
            b = _pool(u, od_pool_w[j], od_pool_b[j], od_pool_scale[j], seq)
            w_out = od_w_out[j]
        h = _outproj_ln(a, b, h, w_out, ln1_g[layer], ln1_b[layer])
        h = _ffn(h, ffn_w_up[layer], ffn_conv_w[layer], ffn_conv_b[layer], ffn_w_down[layer],
                 ln2_g[layer], ln2_b[layer], seq)
    return h.reshape(batch, seq, D_MODEL)
```

```python
import functools
import math

import jax
import jax.numpy as jnp
import numpy as np
from jax import lax
from jax.experimental import pallas as pl
from jax.experimental.pallas import tpu as pltpu

F32 = jnp.float32
BF16 = jnp.bfloat16

D_MODEL = 1024
DEPTH = 4
HALF = D_MODEL // 2
ML_HEADS = 4
ML_HD = HALF // ML_HEADS
FOX_HEADS = 8
FOX_HD = HALF // FOX_HEADS
SSM_HEADDIM = 64
SSM_HEADS = HALF // SSM_HEADDIM
SSM_GROUPS = 2
SSM_STATE = 128
SSM_CONV = 4
SSM_CONV_CH = HALF + 2 * SSM_GROUPS * SSM_STATE
POOL_WINDOWS = (2, 4, 8, 16)
POOL_GW = HALF // len(POOL_WINDOWS)
D_FF = 256 * ((8 * D_MODEL // 3 + 255) // 256)
FFN_CONV = 3
LN_EPS = 1e-5
ALPHA = (2.0 * DEPTH) ** 0.25

LANES = 128
SUBLANES = 8
NEG = -1e30
LOG2E = 1.4426950408889634
VMEM_LIMIT = 56 * 1024 * 1024

ROW_TILE = 512
ML_CHUNK = 256
SSD_CHUNK = 256
FOX_BLOCK = 512
FOX_QUERIES = 512
FOX_UNROLL = 4
FFN_COLS = 256
EPILOGUE_SPLIT = 2


def _params(*sem):
    return pltpu.CompilerParams(dimension_semantics=sem, vmem_limit_bytes=VMEM_LIMIT)


def _dot(a, b):
    return jnp.dot(a, b, preferred_element_type=F32)


def _dot_nt(a, b):
    return lax.dot_general(a, b, (((1,), (1,)), ((), ())), preferred_element_type=F32)


def _split3(x):
    h1 = x.astype(BF16)
    r1 = x - h1.astype(F32)
    h2 = r1.astype(BF16)
    h3 = (r1 - h2.astype(F32)).astype(BF16)
    return h1, h2, h3


def _dot_exact_rhs(t, x):
    h1, h2, h3 = _split3(x)
    return _dot(t, h1) + _dot(t, h2) + _dot(t, h3)


def _dot_exact_lhs(x, t):
    h1, h2, h3 = _split3(x)
    return _dot(h1, t) + _dot(h2, t) + _dot(h3, t)


def _log_sigmoid(x):
    return jnp.minimum(x, 0.0) - jnp.log(1.0 + jnp.exp(-jnp.abs(x)))


def _softplus(x):
    return jnp.maximum(x, 0.0) + jnp.log(1.0 + jnp.exp(-jnp.abs(x)))


def _sigmoid(x):
    return 1.0 / (1.0 + jnp.exp(-x))


def _silu(x):
    return x * _sigmoid(x)


def _tri(n, lower):
    r = lax.broadcasted_iota(jnp.int32, (n, n), 0)
    c = lax.broadcasted_iota(jnp.int32, (n, n), 1)
    return (c <= r) if lower else (r <= c)


def _layer_norm(v, g, b):
    mu = jnp.mean(v, axis=-1, keepdims=True)
    d = v - mu
    var = jnp.mean(d * d, axis=-1, keepdims=True)
    return d * lax.rsqrt(var + LN_EPS) * g + b


EV_MAIN = 4 * HALF
EV_GATES = 2 * ML_HEADS + FOX_HEADS
FOX_QSCALE = FOX_HD ** -0.5 * LOG2E


def _inproj_even_kernel(x_ref, wm_ref, bm_ref, wo_ref, bo_ref, wg_ref, bg_ref, wgt_ref, bgt_ref,
                        wqvt_ref, bqvt_ref, qkv_ref, mo_ref, g_ref, gt_ref, qvt_ref):
    xb = x_ref[...].astype(BF16)
    for j in range(EV_MAIN // HALF):
        sl = slice(j * HALF, (j + 1) * HALF)
        qkv_ref[:, sl] = (_dot(xb, wm_ref[:, sl]) + bm_ref[:, sl]).astype(BF16)
    mo_ref[...] = _dot(xb, wo_ref[...]) + bo_ref[...]
    g_ref[...] = _dot(xb, wg_ref[...]) + bg_ref[...]
    gt_ref[...] = _dot_nt(wgt_ref[...], xb) + bgt_ref[...]
    fqt = (_dot_nt(wqvt_ref[0:HALF, :], xb) + bqvt_ref[0:HALF, :]) * FOX_QSCALE
    qvt_ref[0:HALF, :] = fqt.astype(BF16)
    qvt_ref[HALF:, :] = (_dot_nt(wqvt_ref[HALF:, :], xb) + bqvt_ref[HALF:, :]).astype(BF16)


def _inproj_even(x, w_in, b_in):
    m = x.shape[0]
    tm = ROW_TILE
    o = [0, HALF, 2 * HALF, 3 * HALF, 4 * HALF, 4 * HALF + ML_HEADS, 4 * HALF + 2 * ML_HEADS]
    fq0 = o[6]
    fk0 = fq0 + HALF
    fv0 = fq0 + 2 * HALF
    ff0 = fq0 + 3 * HALF
    wm = jnp.concatenate([w_in[:, :3 * HALF], w_in[:, fk0:fv0]], axis=1).astype(BF16)
    bm = jnp.concatenate([b_in[:3 * HALF], b_in[fk0:fv0]])[None, :]
    wvt = jnp.concatenate([w_in[:, fq0:fk0], w_in[:, fv0:ff0]], axis=1).T.astype(BF16)
    bvt = jnp.concatenate([b_in[fq0:fk0], b_in[fv0:ff0]])[:, None]
    wo = w_in[:, 3 * HALF:4 * HALF].astype(BF16)
    bo = b_in[3 * HALF:4 * HALF][None, :]
    wg16 = jnp.concatenate([w_in[:, o[4]:o[6]], w_in[:, ff0:]], axis=1)
    bg16 = jnp.concatenate([b_in[o[4]:o[6]], b_in[ff0:]])
    wg = jnp.pad(wg16, ((0, 0), (0, LANES - EV_GATES))).astype(BF16)
    bg = jnp.pad(bg16, (0, LANES - EV_GATES))[None, :]
    wgt = wg16.T.astype(BF16)
    bgt = bg16[:, None]
    full = lambda shape: pl.BlockSpec(shape, lambda i: (0, 0))
    return pl.pallas_call(
        _inproj_even_kernel,
        grid=(m // tm,),
        in_specs=[pl.BlockSpec((tm, D_MODEL), lambda i: (i, 0)),
                  full((D_MODEL, EV_MAIN)), full((1, EV_MAIN)),
                  full((D_MODEL, HALF)), full((1, HALF)),
                  full((D_MODEL, LANES)), full((1, LANES)),
                  full((EV_GATES, D_MODEL)), full((EV_GATES, 1)),
                  full((2 * HALF, D_MODEL)), full((2 * HALF, 1))],
        out_specs=[pl.BlockSpec((tm, EV_MAIN), lambda i: (i, 0)),
                   pl.BlockSpec((tm, HALF), lambda i: (i, 0)),
                   pl.BlockSpec((tm, LANES), lambda i: (i, 0)),
                   pl.BlockSpec((EV_GATES, tm), lambda i: (0, i)),
                   pl.BlockSpec((2 * HALF, tm), lambda i: (0, i))],
        out_shape=[jax.ShapeDtypeStruct((m, EV_MAIN), BF16),
                   jax.ShapeDtypeStruct((m, HALF), F32),
                   jax.ShapeDtypeStruct((m, LANES), F32),
                   jax.ShapeDtypeStruct((EV_GATES, m), F32),
                   jax.ShapeDtypeStruct((2 * HALF, m), BF16)],
        compiler_params=_params("arbitrary"),
        name="inproj_even",
    )(x, wm, bm, wo, bo, wg, bg, wgt, bgt, wvt, bvt)


ML_LOG_SCALE = math.log(ML_HD ** -0.5)


FOX_PAIRS = FOX_HEADS // 2
KEXT_W = 2 * LANES
BIAS_PARTS = 3


def _bias_routing():
    r = np.zeros((BIAS_PARTS, LANES, FOX_PAIRS * LANES), np.float32)
    for head in range(FOX_HEADS):
        for part in range(BIAS_PARTS):
            r[part, 2 * ML_HEADS + head, (head // 2) * LANES + BIAS_PARTS * (head % 2) + part] = 1.0
    return jnp.asarray(r, BF16)


def _mlstm_kernel(*refs, lc, nb):
    q_ref, k_ref, v_ref, fk_ref, mo_ref, g_ref = refs[:6]
    gt_refs = refs[6:6 + nb]
    gain_ref, route_ref, h_ref, kext_ref, ct_ref, m_ref, fcar_ref = refs[6 + nb:]
    c = pl.program_id(0)

    @pl.when(c == 0)
    def _():
        ct_ref[...] = jnp.zeros_like(ct_ref)
        m_ref[...] = jnp.zeros_like(m_ref)
        fcar_ref[...] = jnp.zeros_like(fcar_ref)

    tril = _tri(lc, True)
    tril_b = jnp.where(tril, 1.0, 0.0).astype(BF16)
    triu_b = jnp.where(_tri(lc, False), 1.0, 0.0).astype(BF16)
    batches = range(nb)
    g = [g_ref[b] for b in batches]
    gt = [gt_refs[b][...] for b in batches]
    bc_col = [_dot_exact_rhs(tril_b, _log_sigmoid(g[b])) for b in batches]
    bc_row = [_dot_exact_lhs(_log_sigmoid(gt[b]), triu_b) for b in batches]

    for b in batches:
        f_blk = bc_col[b] + fcar_ref[b, 0:1, :]
        fcar_ref[b] = jnp.broadcast_to(f_blk[lc - 1:lc, :], fcar_ref.shape[1:])
        parts = _split3(f_blk * LOG2E)
        bias = sum(_dot(parts[i], route_ref[i]) for i in range(BIAS_PARTS)).astype(BF16)
        for p in range(FOX_PAIRS):
            kext_ref[b, :, p * KEXT_W:p * KEXT_W + LANES] = fk_ref[b, :, p * LANES:(p + 1) * LANES]
            kext_ref[b, :, p * KEXT_W + LANES:(p + 1) * KEXT_W] = bias[:, p * LANES:(p + 1) * LANES]

    lane = lax.broadcasted_iota(jnp.int32, (lc, ML_HD), 1)
    ones_col = jnp.where(lane == 0, 1.0, 0.0).astype(BF16)
    streams = [(b, h) for b in batches for h in range(ML_HEADS)]
    ids = range(len(streams))
    hs = [slice(h * ML_HD, (h + 1) * ML_HD) for _, h in streams]
    vext = [jnp.concatenate([v_ref[b, :, hs[i]], ones_col], axis=1) for i, (b, _) in enumerate(streams)]
    bcl = [bc_col[b][:, ML_HEADS + h:ML_HEADS + h + 1] for b, h in streams]
    bcr = [bc_row[b][ML_HEADS + h:ML_HEADS + h + 1, :] for b, h in streams]
    iil = [g[b][:, h:h + 1] for b, h in streams]
    iir = [gt[b][h:h + 1, :] for b, h in streams]
    m_st = [m_ref[b, h:h + 1, 0:1] for b, h in streams]
    ct = [ct_ref[i] for i in ids]

    s = [_dot_nt(q_ref[b, :, hs[i]], k_ref[b, :, hs[i]]) for i, (b, _) in enumerate(streams)]
    inter = [_dot(q_ref[b, :, hs[i]], ct[i].astype(BF16)) for i, (b, _) in enumerate(streams)]

    for i, (b, h) in enumerate(streams):
        btot = bcr[i][:, lc - 1:lc]
        m_new = jnp.maximum(btot + m_st[i], jnp.max(btot - bcr[i] + iir[i], axis=1, keepdims=True))
        decay = jnp.exp(btot + m_st[i] - m_new)
        ws = jnp.exp(btot - bcl[i] + iil[i] - m_new)
        kwt = (k_ref[b, :, hs[i]].astype(F32) * ws).T.astype(BF16)
        ct_ref[i] = decay * ct[i] + _dot(kwt, vext[i])
        m_ref[b, h:h + 1, :] = jnp.broadcast_to(m_new, (1, LANES))

    mt, w_inter, intra = [], [], []
    for i in ids:
        a_col = bcl[i] + m_st[i]
        dmat = jnp.where(tril, bcl[i] + (iir[i] - bcr[i]), NEG)
        mt.append(jnp.maximum(a_col, jnp.max(dmat, axis=1, keepdims=True)))
        mts = mt[i] - ML_LOG_SCALE
        w_inter.append(jnp.exp(a_col - mts))
        pw = (s[i] * jnp.exp(dmat - mts)).astype(BF16)
        intra.append(_dot(pw, vext[i]))

    for i, (b, _) in enumerate(streams):
        tot = intra[i] + w_inter[i] * inter[i]
        num = tot[:, :ML_HD]
        den = tot[:, ML_HD:ML_HD + 1]
        hh = num / jnp.maximum(jnp.abs(den), jnp.exp(-mt[i]))
        mu = jnp.mean(hh, axis=-1, keepdims=True)
        d = hh - mu
        var = jnp.mean(d * d, axis=-1, keepdims=True)
        hn = d * lax.rsqrt(var + LN_EPS) * gain_ref[:, hs[i]]
        h_ref[b, :, hs[i]] = (_sigmoid(mo_ref[b, :, hs[i]]) * hn).astype(BF16)


def _mlstm(qkv, mo, g, gt, gain, batch, seq):
    lc = min(ML_CHUNK, seq)
    nc = seq // lc
    m = batch * seq
    row = lambda width, j: pl.BlockSpec((batch, lc, width), lambda c, j=j: (0, c, j))
    out = pl.pallas_call(
        functools.partial(_mlstm_kernel, lc=lc, nb=batch),
        grid=(nc,),
        in_specs=[row(HALF, 0), row(HALF, 1), row(HALF, 2), row(HALF, 3), row(HALF, 0), row(LANES, 0)]
        + [pl.BlockSpec((EV_GATES, lc), lambda c, b=b: (0, b * nc + c)) for b in range(batch)]
        + [pl.BlockSpec((1, HALF), lambda c: (0, 0)),
           pl.BlockSpec((BIAS_PARTS, LANES, FOX_PAIRS * LANES), lambda c: (0, 0, 0))],
        out_specs=[row(HALF, 0), row(FOX_PAIRS * KEXT_W, 0)],
        out_shape=[jax.ShapeDtypeStruct((batch, seq, HALF), BF16),
                   jax.ShapeDtypeStruct((batch, seq, FOX_PAIRS * KEXT_W), BF16)],
        scratch_shapes=[pltpu.VMEM((batch * ML_HEADS, ML_HD, 2 * ML_HD), F32),
                        pltpu.VMEM((batch, SUBLANES, LANES), F32),
                        pltpu.VMEM((batch, SUBLANES, LANES), F32)],
        compiler_params=_params("arbitrary"),
        name="mlstm",
    )(*([qkv.reshape(batch, seq, -1)] * 4), mo.reshape(batch, seq, HALF), g.reshape(batch, seq, LANES),
      *([gt] * batch), gain, _bias_routing())
    return out[0].reshape(m, HALF), out[1].reshape(m, FOX_PAIRS * KEXT_W)


BF16_ROWS = 2 * SUBLANES
FOX_ACC_ROWS = FOX_HD + BF16_ROWS


def _fox_kernel(qt_ref, kext_ref, vt_ref, o_ref, m_sc, acc_sc, s0_sc, s1_sc, mx0_sc, mx1_sc, *, blk, qw):
    ratio = qw // blk
    qi = pl.program_id(2)
    q2 = qt_ref[...]
    row = lax.broadcasted_iota(jnp.int32, (LANES, qw), 0)
    top = row < FOX_HD
    zero = jnp.zeros_like(q2)
    qe = []
    for hd in range(2):
        pick = (row >= BIAS_PARTS * hd) & (row < BIAS_PARTS * (hd + 1))
        minus1 = jnp.where(pick, -1.0, 0.0).astype(BF16)
        qh = jnp.where(top, q2, zero) if hd == 0 else jnp.where(top, zero, q2)
        qe.append(jnp.concatenate([qh, minus1], axis=0))

    m_sc[...] = jnp.full_like(m_sc, NEG)
    acc_sc[...] = jnp.zeros_like(acc_sc)
    s_bufs = (s0_sc, s1_sc)
    mx_bufs = (mx0_sc, mx1_sc)
    ones_rows = jnp.where(lax.broadcasted_iota(jnp.int32, (BF16_ROWS, blk), 0) == 0, 1.0, 0.0).astype(BF16)

    def scores(kj, slot, diag):
        k0 = pl.multiple_of(kj * blk, blk)
        ke = kext_ref[pl.ds(k0, blk), :]
        for hd in range(2):
            s = _dot(ke, qe[hd])
            if diag is not None:
                key = lax.broadcasted_iota(jnp.int32, (blk, qw), 0) + diag * blk
                s = jnp.where(key <= lax.broadcasted_iota(jnp.int32, (blk, qw), 1), s, NEG)
            s_bufs[slot][hd] = s
            mx_bufs[slot][hd] = jnp.max(s, axis=0, keepdims=True)

    def accumulate(kj, slot):
        k0 = pl.multiple_of(kj * blk, blk)
        for hd in range(2):
            rows = slice(hd * FOX_HD, (hd + 1) * FOX_HD)
            m_prev = m_sc[hd]
            m_next = jnp.maximum(m_prev, mx_bufs[slot][hd])
            alpha = jnp.exp2(m_prev - m_next)
            p = jnp.exp2(s_bufs[slot][hd] - m_next).astype(BF16)
            m_sc[hd] = m_next
            vt1 = jnp.concatenate([vt_ref[rows, pl.ds(k0, blk)], ones_rows], axis=0)
            acc_sc[hd] = alpha * acc_sc[hd] + _dot(vt1, p)

    n_off = ratio * qi

    def key_block(t):
        return jnp.where(t < ratio, ratio * qi + t, t - ratio)

    def step(t, slot, diag=None):
        scores(key_block(t), slot, diag)
        accumulate(key_block(t - 1), 1 - slot)

    scores(ratio * qi, 0, 0)
    for r in range(1, ratio):
        step(r, r % 2, r)

    def body(i, carry):
        for u in range(FOX_UNROLL):
            step(ratio + FOX_UNROLL * i + u, (ratio + u) % 2)
        return carry

    lax.fori_loop(0, n_off // FOX_UNROLL, body, 0)
    done = (n_off // FOX_UNROLL) * FOX_UNROLL
    width = FOX_UNROLL // 2
    while width >= 1:
        if width % math.gcd(ratio, FOX_UNROLL) == 0:
            @pl.when(n_off & width != 0)
            def _(done=done, width=width):
                for u in range(width):
                    step(ratio + done + u, (ratio + u) % 2)
            done = done + (n_off & width)
        width //= 2

    last = ratio + n_off - 1
    if ratio % 2 == 0:
        accumulate(key_block(last), (ratio - 1) % 2)
    else:
        for parity in range(2):
            @pl.when(last % 2 == parity)
            def _(parity=parity):
                accumulate(key_block(last), parity)

    out = [acc_sc[hd, 0:FOX_HD, :] * (1.0 / acc_sc[hd, FOX_HD:FOX_HD + 1, :]) for hd in range(2)]
    o_ref[...] = jnp.concatenate(out, axis=0).T.astype(BF16)


def _fox(qvt, kext, batch, seq):
    blk = min(FOX_BLOCK, seq)
    qw = min(FOX_QUERIES, seq)
    nq = seq // qw
    m = batch * seq
    return pl.pallas_call(
        functools.partial(_fox_kernel, blk=blk, qw=qw),
        grid=(batch, FOX_PAIRS, nq),
        in_specs=[pl.BlockSpec((LANES, qw), lambda b, p, i: (p, b * nq + i)),
                  pl.BlockSpec((seq, KEXT_W), lambda b, p, i: (b, p)),
                  pl.BlockSpec((LANES, seq), lambda b, p, i: (FOX_PAIRS + p, b))],
        out_specs=pl.BlockSpec((qw, LANES), lambda b, p, i: (b * nq + i, p)),
        out_shape=jax.ShapeDtypeStruct((m, HALF), BF16),
        scratch_shapes=[pltpu.VMEM((2, 1, qw), F32),
                        pltpu.VMEM((2, FOX_ACC_ROWS, qw), F32),
                        pltpu.VMEM((2, blk, qw), F32),
                        pltpu.VMEM((2, blk, qw), F32),
                        pltpu.VMEM((2, 1, qw), F32),
                        pltpu.VMEM((2, 1, qw), F32)],
        compiler_params=_params("arbitrary", "arbitrary", "arbitrary"),
        name="fox_attention",
    )(qvt, kext, qvt)


def _outproj_kernel(a_ref, b_ref, x_ref, w_ref, g_ref, beta_ref, o_ref):
    tm = x_ref.shape[0]
    for r in range(EPILOGUE_SPLIT):
        rows = slice(r * tm // EPILOGUE_SPLIT, (r + 1) * tm // EPILOGUE_SPLIT)
        ab = jnp.concatenate([a_ref[rows, :], b_ref[rows, :]], axis=1)
        y = _dot(ab, w_ref[...])
        o_ref[rows, :] = _layer_norm(ALPHA * x_ref[rows, :] + y, g_ref[...], beta_ref[...])


def _outproj_ln(a, b, x, w_out, ln_g, ln_b):
    m = x.shape[0]
    tm = ROW_TILE
    row = lambda width: pl.BlockSpec((tm, width), lambda i: (i, 0))
    full = lambda shape: pl.BlockSpec(shape, lambda i: (0, 0))
    return pl.pallas_call(
        _outproj_kernel,
        grid=(m // tm,),
        in_specs=[row(HALF), row(HALF), row(D_MODEL), full((D_MODEL, D_MODEL)),
                  full((1, D_MODEL)), full((1, D_MODEL))],
        out_specs=row(D_MODEL),
        out_shape=jax.ShapeDtypeStruct((m, D_MODEL), F32),
        compiler_params=_params("arbitrary"),
        name="outproj_ln",
    )(a, b, x, w_out.astype(BF16), ln_g[None, :], ln_b[None, :])


def _ffn_kernel(x_ref, wu_ref, cw_ref, cb_ref, wd_ref, g_ref, beta_ref, o_ref,
                he_ref, hprev_ref, act_ref, *, tm, tiles_per_seq):
    i = pl.program_id(0)

    @pl.when(i % tiles_per_seq == 0)
    def _():
        hprev_ref[...] = jnp.zeros_like(hprev_ref)

    xb = x_ref[...].astype(BF16)
    cw = FFN_COLS

    def conv_cols(cs):
        he_ref[0:SUBLANES, :] = hprev_ref[:, cs]
        he_ref[SUBLANES:, :] = _dot(xb, wu_ref[:, cs])
        hprev_ref[:, cs] = he_ref[tm:tm + SUBLANES, :]
        he = he_ref[...]
        conv = cb_ref[:, cs] + cw_ref[FFN_CONV - 1:FFN_CONV, cs] * he[SUBLANES:, :]
        for back in range(1, FFN_CONV):
            conv = conv + cw_ref[FFN_CONV - 1 - back:FFN_CONV - back, cs] * pltpu.roll(he, back, 0)[SUBLANES:, :]
        return conv

    for j in range(D_FF // cw):
        val = conv_cols(slice(j * cw, (j + 1) * cw))
        gate = conv_cols(slice(D_FF + j * cw, D_FF + (j + 1) * cw))
        act_ref[:, j * cw:(j + 1) * cw] = (_silu(gate) * val).astype(BF16)
    for r in range(EPILOGUE_SPLIT):
        rows = slice(r * tm // EPILOGUE_SPLIT, (r + 1) * tm // EPILOGUE_SPLIT)
        f = _dot(act_ref[rows, :], wd_ref[...])
        o_ref[rows, :] = _layer_norm(ALPHA * x_ref[rows, :] + f, g_ref[...], beta_ref[...])


def _ffn(x, w_up, conv_w, conv_b, w_down, ln_g, ln_b, seq):
    m = x.shape[0]
    tm = min(ROW_TILE, seq)
    full = lambda shape: pl.BlockSpec(shape, lambda i: (0, 0))
    return pl.pallas_call(
        functools.partial(_ffn_kernel, tm=tm, tiles_per_seq=seq // tm),
        grid=(m // tm,),
        in_specs=[pl.BlockSpec((tm, D_MODEL), lambda i: (i, 0)),
                  full((D_MODEL, 2 * D_FF)), full((FFN_CONV, 2 * D_FF)), full((1, 2 * D_FF)),
                  full((D_FF, D_MODEL)), full((1, D_MODEL)), full((1, D_MODEL))],
        out_specs=pl.BlockSpec((tm, D_MODEL), lambda i: (i, 0)),
        out_shape=jax.ShapeDtypeStruct((m, D_MODEL), F32),
        scratch_shapes=[pltpu.VMEM((tm + SUBLANES, FFN_COLS), F32),
                        pltpu.VMEM((SUBLANES, 2 * D_FF), F32),
                        pltpu.VMEM((tm, D_FF), BF16)],
        compiler_params=_params("arbitrary"),
        name="conv_ffn",
    )(x, w_up.astype(BF16), conv_w, conv_b[None, :], w_down.astype(BF16), ln_g[None, :], ln_b[None, :])


OD_MAIN = SSM_CONV_CH + 2 * HALF


def _inproj_odd_kernel(x_ref, wm_ref, wd_ref, wdt_ref, u_ref, dt_ref, dtt_ref):
    xb = x_ref[...].astype(BF16)
    for j in range(OD_MAIN // HALF):
        sl = slice(j * HALF, (j + 1) * HALF)
        u_ref[:, sl] = _dot(xb, wm_ref[:, sl])
    dt_ref[...] = _dot(xb, wd_ref[...])
    dtt_ref[...] = _dot_nt(wdt_ref[...], xb)


def _inproj_odd(x, w_in):
    m = x.shape[0]
    tm = ROW_TILE
    z1 = HALF
    x1 = z1 + SSM_CONV_CH
    d1 = x1 + SSM_HEADS
    wm = jnp.concatenate([w_in[:, z1:x1], w_in[:, :z1], w_in[:, d1:]], axis=1).astype(BF16)
    wd8 = w_in[:, x1:d1]
    wd = jnp.pad(wd8, ((0, 0), (0, LANES - SSM_HEADS))).astype(BF16)
    wdt = wd8.T.astype(BF16)
    full = lambda shape: pl.BlockSpec(shape, lambda i: (0, 0))
    return pl.pallas_call(
        _inproj_odd_kernel,
        grid=(m // tm,),
        in_specs=[pl.BlockSpec((tm, D_MODEL), lambda i: (i, 0)),
                  full((D_MODEL, OD_MAIN)), full((D_MODEL, LANES)), full((SSM_HEADS, D_MODEL))],
        out_specs=[pl.BlockSpec((tm, OD_MAIN), lambda i: (i, 0)),
                   pl.BlockSpec((tm, LANES), lambda i: (i, 0)),
                   pl.BlockSpec((SSM_HEADS, tm), lambda i: (0, i))],
        out_shape=[jax.ShapeDtypeStruct((m, OD_MAIN), F32),
                   jax.ShapeDtypeStruct((m, LANES), F32),
                   jax.ShapeDtypeStruct((SSM_HEADS, m), F32)],
        compiler_params=_params("arbitrary"),
        name="inproj_odd",
    )(x, wm, wd, wdt)


SSM_HPG = SSM_HEADS // SSM_GROUPS
GROUP_W = SSM_HPG * SSM_HEADDIM


def _ssd_kernel(xbc_ref, z_ref, dt_ref, dtt_ref, cw_ref, cb_ref, dtb_ref, dtbt_ref, a_ref, at_ref,
                dskip_ref, gain_ref, y_ref, xe_ref, ht_ref, *, lc):
    c = pl.program_id(1)

    @pl.when(c == 0)
    def _():
        xe_ref[0:SUBLANES, :] = jnp.zeros((SUBLANES, SSM_CONV_CH), F32)
        ht_ref[...] = jnp.zeros_like(ht_ref)

    xe_ref[SUBLANES:, :] = xbc_ref[...]
    xe = xe_ref[...]
    conv = cb_ref[...] + cw_ref[SSM_CONV - 1:SSM_CONV, :] * xe[SUBLANES:, :]
    for back in range(1, SSM_CONV):
        conv = conv + cw_ref[SSM_CONV - 1 - back:SSM_CONV - back, :] * pltpu.roll(xe, back, 0)[SUBLANES:, :]
    xe_ref[0:SUBLANES, :] = xe_ref[lc:lc + SUBLANES, :]
    xact = _silu(conv)
    xs = xact[:, :HALF]

    tril = _tri(lc, True)
    tril_b = jnp.where(tril, 1.0, 0.0).astype(BF16)
    triu_b = jnp.where(_tri(lc, False), 1.0, 0.0).astype(BF16)
    er = lax.broadcasted_iota(jnp.int32, (LANES, HALF), 0)
    ec = lax.broadcasted_iota(jnp.int32, (LANES, HALF), 1)
    expand = jnp.where(jnp.right_shift(ec, 6) == er, 1.0, 0.0).astype(BF16)

    dt_col = _softplus(dt_ref[...] + dtb_ref[...])
    dt_row = _softplus(dtt_ref[...] + dtbt_ref[...])
    acs_col = _dot_exact_rhs(tril_b, dt_col * a_ref[...])
    acs_row = _dot_exact_lhs(dt_row * at_ref[...], triu_b)
    acs_x = _dot_exact_lhs(acs_col, expand)
    dt_x = _dot_exact_lhs(dt_col, expand)
    xdt = xs * dt_x
    left = lax.broadcasted_iota(jnp.int32, (lc, LANES), 1) < SSM_HEADDIM

    groups = range(SSM_GROUPS)
    gs = [slice(gi * GROUP_W, (gi + 1) * GROUP_W) for gi in groups]
    bm = [xact[:, HALF + gi * SSM_STATE:HALF + (gi + 1) * SSM_STATE] for gi in groups]
    cmb = [xact[:, HALF + (SSM_GROUPS + gi) * SSM_STATE:HALF + (SSM_GROUPS + gi + 1) * SSM_STATE].astype(BF16)
           for gi in groups]
    ht = [ht_ref[gi] for gi in groups]
    cbm = [_dot_nt(cmb[gi], bm[gi].astype(BF16)) for gi in groups]
    off = [_dot(cmb[gi], ht[gi].astype(BF16)) for gi in groups]
    for gi in groups:
        acs_g = acs_x[:, gs[gi]]
        last = acs_g[lc - 1:lc, :]
        xw = (xdt[:, gs[gi]] * jnp.exp(last - acs_g)).astype(BF16)
        ht_ref[gi] = jnp.exp(last) * ht[gi] + _dot(bm[gi].T.astype(BF16), xw)
    ys = []
    for gi in groups:
        pair_out = []
        for pj in range(SSM_HPG // 2):
            lo = gi * GROUP_W + pj * LANES
            xpair = xdt[:, lo:lo + LANES].astype(BF16)
            yh = []
            for e in (gi * SSM_HPG + 2 * pj, gi * SSM_HPG + 2 * pj + 1):
                seg = jnp.where(tril, acs_col[:, e:e + 1] - acs_row[e:e + 1, :], NEG)
                yh.append(_dot((cbm[gi] * jnp.exp(seg)).astype(BF16), xpair))
            pair_out.append(jnp.where(left, yh[0], yh[1]))
        y_diag = jnp.concatenate(pair_out, axis=1)
        ys.append(y_diag + off[gi] * jnp.exp(acs_x[:, gs[gi]]))
    y = jnp.concatenate(ys, axis=1) + dskip_ref[...] * xs
    gated = y * _silu(z_ref[...])
    ms = jnp.mean(gated * gated, axis=-1, keepdims=True)
    y_ref[...] = (gated * lax.rsqrt(ms + LN_EPS) * gain_ref[...]).astype(BF16)


def _ssd(u, dt, dtt, conv_w, conv_b, dt_bias, a_log, d_skip, ssm_norm, batch, seq):
    lc = min(SSD_CHUNK, seq)
    nc = seq // lc
    m = batch * seq
    a = -jnp.exp(a_log.astype(F32))
    pad = lambda v: jnp.pad(v, (0, LANES - SSM_HEADS))[None, :]
    full = lambda shape: pl.BlockSpec(shape, lambda b, c: (0, 0))
    return pl.pallas_call(
        functools.partial(_ssd_kernel, lc=lc),
        grid=(batch, nc),
        in_specs=[pl.BlockSpec((lc, SSM_CONV_CH), lambda b, c: (b * nc + c, 0)),
                  pl.BlockSpec((lc, HALF), lambda b, c: (b * nc + c, SSM_CONV_CH // HALF)),
                  pl.BlockSpec((lc, LANES), lambda b, c: (b * nc + c, 0)),
                  pl.BlockSpec((SSM_HEADS, lc), lambda b, c: (0, b * nc + c)),
                  full((SSM_CONV, SSM_CONV_CH)), full((1, SSM_CONV_CH)),
                  full((1, LANES)), full((SSM_HEADS, 1)), full((1, LANES)), full((SSM_HEADS, 1)),
                  full((1, HALF)), full((1, HALF))],
        out_specs=pl.BlockSpec((lc, HALF), lambda b, c: (b * nc + c, 0)),
        out_shape=jax.ShapeDtypeStruct((m, HALF), BF16),
        scratch_shapes=[pltpu.VMEM((lc + SUBLANES, SSM_CONV_CH), F32),
                        pltpu.VMEM((SSM_GROUPS, SSM_STATE, GROUP_W), F32)],
        compiler_params=_params("arbitrary", "arbitrary"),
        name="ssd",
    )(u, u, dt, dtt, conv_w, conv_b[None, :], pad(dt_bias), dt_bias[:, None], pad(a), a[:, None],
      jnp.repeat(d_skip, SSM_HEADDIM)[None, :], ssm_norm[None, :])


POOL_HALO = 16


def _pool_kernel(u_ref, w_ref, b_ref, sc_ref, p_ref, ue_ref, *, tm, tiles_per_seq):
    i = pl.program_id(0)

    @pl.when(i % tiles_per_seq == 0)
    def _():
        ue_ref[0:POOL_HALO, :] = jnp.zeros((POOL_HALO, HALF), F32)

    ue_ref[POOL_HALO:, :] = u_ref[...]
    t = (i % tiles_per_seq) * tm + lax.broadcasted_iota(jnp.int32, (tm, 1), 0)
    for gi, win in enumerate(POOL_WINDOWS):
        gs = slice(gi * POOL_GW, (gi + 1) * POOL_GW)
        acc = ue_ref[:, gs]
        span = 1
        while span < win:
            acc = acc + pltpu.roll(acc, span, 0)
            span *= 2
        cnt = jnp.minimum(t + 1, win).astype(F32)
        pooled = acc[POOL_HALO:, :] / cnt - ue_ref[POOL_HALO:POOL_HALO + tm, gs]
        y = _dot(pooled.astype(BF16), w_ref[gi]) + b_ref[:, gs]
        p_ref[:, gs] = (y * sc_ref[:, gs]).astype(BF16)
    ue_ref[0:POOL_HALO, :] = ue_ref[tm:tm + POOL_HALO, :]


def _pool(u, pool_w, pool_b, pool_scale, seq):
    m = u.shape[0]
    tm = min(ROW_TILE, seq)
    full2 = lambda shape: pl.BlockSpec(shape, lambda i: (0, 0))
    return pl.pallas_call(
        functools.partial(_pool_kernel, tm=tm, tiles_per_seq=seq // tm),
        grid=(m // tm,),
        in_specs=[pl.BlockSpec((tm, HALF), lambda i: (i, (SSM_CONV_CH + HALF) // HALF)),
                  pl.BlockSpec((len(POOL_WINDOWS), POOL_GW, POOL_GW), lambda i: (0, 0, 0)),
                  full2((1, HALF)), full2((1, HALF))],
        out_specs=pl.BlockSpec((tm, HALF), lambda i: (i, 0)),
        out_shape=jax.ShapeDtypeStruct((m, HALF), BF16),
        scratch_shapes=[pltpu.VMEM((tm + POOL_HALO, HALF), F32)],
        compiler_params=_params("arbitrary"),
        name="pool",
    )(u, pool_w.astype(BF16), pool_b[None, :], pool_scale[None, :])


def kernel(x, ev_w_in, ev_b_in, ev_ml_norm, ev_w_out, od_w_in, od_conv_w, od_conv_b, od_dt_bias, od_a_log,
           od_d_skip, od_ssm_norm, od_pool_w, od_pool_b, od_pool_scale, od_w_out, ffn_w_up, ffn_conv_w,
           ffn_conv_b, ffn_w_down, ln1_g, ln1_b, ln2_g, ln2_b):
    batch, seq, _ = x.shape
    h = x.reshape(batch * seq, D_MODEL)
    for layer in range(DEPTH):
        j = layer // 2
        if layer % 2 == 0:
            qkv, mo, g, gt, qvt = _inproj_even(h, ev_w_in[j], ev_b_in[j])
            a, kext = _mlstm(qkv, mo, g, gt, ev_ml_norm[j][None, :], batch, seq)
            b = _fox(qvt, kext, batch, seq)
            w_out = ev_w_out[j]
        else:
            u, dt, dtt = _inproj_odd(h, od_w_in[j])
            a = _ssd(u, dt, dtt, od_conv_w[j], od_conv_b[j], od_dt_bias[j], od_a_log[j], od_d_skip[j],
                     od_ssm_norm[j], batch, seq)
            b = _pool(u, od_pool_w[j], od_pool_b[j], od_pool_scale[j], seq)
            w_out = od_w_out[j]
        h = _outproj_ln(a, b, h, w_out, ln1_g[layer], ln1_b[layer])
        h = _ffn(h, ffn_w_up[layer], ffn_conv_w[layer], ffn_conv_b[layer], ffn_w_down[layer],
                 ln2_g[layer], ln2_b[layer], seq)
    return h.reshape(batch, seq, D_MODEL)
```

```python
import functools
import math

import jax
import jax.numpy as jnp
import numpy as np
from jax import lax
from jax.experimental import pallas as pl
from jax.experimental.pallas import tpu as pltpu

F32 = jnp.float32
BF16 = jnp.bfloat16

D_MODEL = 1024
DEPTH = 4
HALF = D_MODEL // 2
ML_HEADS = 4
ML_HD = HALF // ML_HEADS
FOX_HEADS = 8
FOX_HD = HALF // FOX_HEADS
SSM_HEADDIM = 64
SSM_HEADS = HALF // SSM_HEADDIM
SSM_GROUPS = 2
SSM_STATE = 128
SSM_CONV = 4
SSM_CONV_CH = HALF + 2 * SSM_GROUPS * SSM_STATE
POOL_WINDOWS = (2, 4, 8, 16)
POOL_GW = HALF // len(POOL_WINDOWS)
D_FF = 256 * ((8 * D_MODEL // 3 + 255) // 256)
FFN_CONV = 3
LN_EPS = 1e-5
ALPHA = (2.0 * DEPTH) ** 0.25

LANES = 128
SUBLANES = 8
NEG = -1e30
LOG2E = 1.4426950408889634
VMEM_LIMIT = 56 * 1024 * 1024

ROW_TILE = 512
ML_CHUNK = 512
SSD_CHUNK = 256
FOX_BLOCK = 256
FOX_QUERIES = 512
FOX_UNROLL = 8
FFN_COLS = 256
EPILOGUE_SPLIT = 2


def _params(*sem):
    return pltpu.CompilerParams(dimension_semantics=sem, vmem_limit_bytes=VMEM_LIMIT)


def _dot(a, b):
    return jnp.dot(a, b, preferred_element_type=F32)


def _dot_nt(a, b):
    return lax.dot_general(a, b, (((1,), (1,)), ((), ())), preferred_element_type=F32)


def _split3(x):
    h1 = x.astype(BF16)
    r1 = x - h1.astype(F32)
    h2 = r1.astype(BF16)
    h3 = (r1 - h2.astype(F32)).astype(BF16)
    return h1, h2, h3


def _dot_exact_rhs(t, x):
    h1, h2, h3 = _split3(x)
    return _dot(t, h1) + _dot(t, h2) + _dot(t, h3)


def _dot_exact_lhs(x, t):
    h1, h2, h3 = _split3(x)
    return _dot(h1, t) + _dot(h2, t) + _dot(h3, t)


def _log_sigmoid(x):
    return jnp.minimum(x, 0.0) - jnp.log(1.0 + jnp.exp(-jnp.abs(x)))


def _softplus(x):
    return jnp.maximum(x, 0.0) + jnp.log(1.0 + jnp.exp(-jnp.abs(x)))


def _sigmoid(x):
    return 1.0 / (1.0 + jnp.exp(-x))


def _silu(x):
    return x * _sigmoid(x)


def _tri(n, lower):
    r = lax.broadcasted_iota(jnp.int32, (n, n), 0)
    c = lax.broadcasted_iota(jnp.int32, (n, n), 1)
    return (c <= r) if lower else (r <= c)


def _layer_norm(v, g, b):
    mu = jnp.mean(v, axis=-1, keepdims=True)
    d = v - mu
    var = jnp.mean(d * d, axis=-1, keepdims=True)
    return d * lax.rsqrt(var + LN_EPS) * g + b


EV_MAIN = 4 * HALF
EV_GATES = 2 * ML_HEADS + FOX_HEADS
FOX_QSCALE = FOX_HD ** -0.5 * LOG2E


def _inproj_even_kernel(x_ref, wm_ref, bm_ref, wo_ref, bo_ref, wg_ref, bg_ref, wgt_ref, bgt_ref,
                        wqvt_ref, bqvt_ref, qkv_ref, mo_ref, g_ref, gt_ref, qvt_ref):
    xb = x_ref[...].astype(BF16)
    for j in range(EV_MAIN // HALF):
        sl = slice(j * HALF, (j + 1) * HALF)
        qkv_ref[:, sl] = (_dot(xb, wm_ref[:, sl]) + bm_ref[:, sl]).astype(BF16)
    mo_ref[...] = _dot(xb, wo_ref[...]) + bo_ref[...]
    g_ref[...] = _dot(xb, wg_ref[...]) + bg_ref[...]
    gt_ref[...] = _dot_nt(wgt_ref[...], xb) + bgt_ref[...]
    fqt = (_dot_nt(wqvt_ref[0:HALF, :], xb) + bqvt_ref[0:HALF, :]) * FOX_QSCALE
    qvt_ref[0:HALF, :] = fqt.astype(BF16)
    qvt_ref[HALF:, :] = (_dot_nt(wqvt_ref[HALF:, :], xb) + bqvt_ref[HALF:, :]).astype(BF16)


def _inproj_even(x, w_in, b_in):
    m = x.shape[0]
    tm = ROW_TILE
    o = [0, HALF, 2 * HALF, 3 * HALF, 4 * HALF, 4 * HALF + ML_HEADS, 4 * HALF + 2 * ML_HEADS]
    fq0 = o[6]
    fk0 = fq0 + HALF
    fv0 = fq0 + 2 * HALF
    ff0 = fq0 + 3 * HALF
    wm = jnp.concatenate([w_in[:, :3 * HALF], w_in[:, fk0:fv0]], axis=1).astype(BF16)
    bm = jnp.concatenate([b_in[:3 * HALF], b_in[fk0:fv0]])[None, :]
    wvt = jnp.concatenate([w_in[:, fq0:fk0], w_in[:, fv0:ff0]], axis=1).T.astype(BF16)
    bvt = jnp.concatenate([b_in[fq0:fk0], b_in[fv0:ff0]])[:, None]
    wo = w_in[:, 3 * HALF:4 * HALF].astype(BF16)
    bo = b_in[3 * HALF:4 * HALF][None, :]
    wg16 = jnp.concatenate([w_in[:, o[4]:o[6]], w_in[:, ff0:]], axis=1)
    bg16 = jnp.concatenate([b_in[o[4]:o[6]], b_in[ff0:]])
    wg = jnp.pad(wg16, ((0, 0), (0, LANES - EV_GATES))).astype(BF16)
    bg = jnp.pad(bg16, (0, LANES - EV_GATES))[None, :]
    wgt = wg16.T.astype(BF16)
    bgt = bg16[:, None]
    full = lambda shape: pl.BlockSpec(shape, lambda i: (0, 0))
    return pl.pallas_call(
        _inproj_even_kernel,
        grid=(m // tm,),
        in_specs=[pl.BlockSpec((tm, D_MODEL), lambda i: (i, 0)),
                  full((D_MODEL, EV_MAIN)), full((1, EV_MAIN)),
                  full((D_MODEL, HALF)), full((1, HALF)),
                  full((D_MODEL, LANES)), full((1, LANES)),
                  full((EV_GATES, D_MODEL)), full((EV_GATES, 1)),
                  full((2 * HALF, D_MODEL)), full((2 * HALF, 1))],
        out_specs=[pl.BlockSpec((tm, EV_MAIN), lambda i: (i, 0)),
                   pl.BlockSpec((tm, HALF), lambda i: (i, 0)),
                   pl.BlockSpec((tm, LANES), lambda i: (i, 0)),
                   pl.BlockSpec((EV_GATES, tm), lambda i: (0, i)),
                   pl.BlockSpec((2 * HALF, tm), lambda i: (0, i))],
        out_shape=[jax.ShapeDtypeStruct((m, EV_MAIN), BF16),
                   jax.ShapeDtypeStruct((m, HALF), F32),
                   jax.ShapeDtypeStruct((m, LANES), F32),
                   jax.ShapeDtypeStruct((EV_GATES, m), F32),
                   jax.ShapeDtypeStruct((2 * HALF, m), BF16)],
        compiler_params=_params("arbitrary"),
        name="inproj_even",
    )(x, wm, bm, wo, bo, wg, bg, wgt, bgt, wvt, bvt)


ML_LOG_SCALE = math.log(ML_HD ** -0.5)


FOX_PAIRS = FOX_HEADS // 2
KEXT_W = 2 * LANES
BIAS_PARTS = 3


def _bias_routing():
    r = np.zeros((BIAS_PARTS, LANES, FOX_PAIRS * LANES), np.float32)
    for head in range(FOX_HEADS):
        for part in range(BIAS_PARTS):
            r[part, 2 * ML_HEADS + head, (head // 2) * LANES + BIAS_PARTS * (head % 2) + part] = 1.0
    return jnp.asarray(r, BF16)


def _mlstm_kernel(*refs, lc, nb):
    q_ref, k_ref, v_ref, fk_ref, mo_ref, g_ref = refs[:6]
    gt_refs = refs[6:6 + nb]
    gain_ref, route_ref, h_ref, kext_ref, ct_ref, m_ref, fcar_ref = refs[6 + nb:]
    c = pl.program_id(0)

    @pl.when(c == 0)
    def _():
        ct_ref[...] = jnp.zeros_like(ct_ref)
        m_ref[...] = jnp.zeros_like(m_ref)
        fcar_ref[...] = jnp.zeros_like(fcar_ref)

    tril = _tri(lc, True)
    tril_b = jnp.where(tril, 1.0, 0.0).astype(BF16)
    triu_b = jnp.where(_tri(lc, False), 1.0, 0.0).astype(BF16)
    batches = range(nb)
    g = [g_ref[b] for b in batches]
    gt = [gt_refs[b][...] for b in batches]
    bc_col = [_dot_exact_rhs(tril_b, _log_sigmoid(g[b])) for b in batches]
    bc_row = [_dot_exact_lhs(_log_sigmoid(gt[b]), triu_b) for b in batches]

    for b in batches:
        f_blk = bc_col[b] + fcar_ref[b, 0:1, :]
        fcar_ref[b] = jnp.broadcast_to(f_blk[lc - 1:lc, :], fcar_ref.shape[1:])
        parts = _split3(f_blk * LOG2E)
        bias = sum(_dot(parts[i], route_ref[i]) for i in range(BIAS_PARTS)).astype(BF16)
        for p in range(FOX_PAIRS):
            kext_ref[b, :, p * KEXT_W:p * KEXT_W + LANES] = fk_ref[b, :, p * LANES:(p + 1) * LANES]
            kext_ref[b, :, p * KEXT_W + LANES:(p + 1) * KEXT_W] = bias[:, p * LANES:(p + 1) * LANES]

    lane = lax.broadcasted_iota(jnp.int32, (lc, ML_HD), 1)
    ones_col = jnp.where(lane == 0, 1.0, 0.0).astype(BF16)
    streams = [(b, h) for b in batches for h in range(ML_HEADS)]
    ids = range(len(streams))
    hs = [slice(h * ML_HD, (h + 1) * ML_HD) for _, h in streams]
    vext = [jnp.concatenate([v_ref[b, :, hs[i]], ones_col], axis=1) for i, (b, _) in enumerate(streams)]
    bcl = [bc_col[b][:, ML_HEADS + h:ML_HEADS + h + 1] for b, h in streams]
    bcr = [bc_row[b][ML_HEADS + h:ML_HEADS + h + 1, :] for b, h in streams]
    iil = [g[b][:, h:h + 1] for b, h in streams]
    iir = [gt[b][h:h + 1, :] for b, h in streams]
    m_st = [m_ref[b, h:h + 1, 0:1] for b, h in streams]
    ct = [ct_ref[i] for i in ids]

    s = [_dot_nt(q_ref[b, :, hs[i]], k_ref[b, :, hs[i]]) for i, (b, _) in enumerate(streams)]
    inter = [_dot(q_ref[b, :, hs[i]], ct[i].astype(BF16)) for i, (b, _) in enumerate(streams)]

    for i, (b, h) in enumerate(streams):
        btot = bcr[i][:, lc - 1:lc]
        m_new = jnp.maximum(btot + m_st[i], jnp.max(btot - bcr[i] + iir[i], axis=1, keepdims=True))
        decay = jnp.exp(btot + m_st[i] - m_new)
        ws = jnp.exp(btot - bcl[i] + iil[i] - m_new)
        kwt = (k_ref[b, :, hs[i]].astype(F32) * ws).T.astype(BF16)
        ct_ref[i] = decay * ct[i] + _dot(kwt, vext[i])
        m_ref[b, h:h + 1, :] = jnp.broadcast_to(m_new, (1, LANES))

    mt, w_inter, intra = [], [], []
    for i in ids:
        a_col = bcl[i] + m_st[i]
        dmat = jnp.where(tril, bcl[i] + (iir[i] - bcr[i]), NEG)
        mt.append(jnp.maximum(a_col, jnp.max(dmat, axis=1, keepdims=True)))
        mts = mt[i] - ML_LOG_SCALE
        w_inter.append(jnp.exp(a_col - mts))
        pw = (s[i] * jnp.exp(dmat - mts)).astype(BF16)
        intra.append(_dot(pw, vext[i]))

    for i, (b, _) in enumerate(streams):
        tot = intra[i] + w_inter[i] * inter[i]
        num = tot[:, :ML_HD]
        den = tot[:, ML_HD:ML_HD + 1]
        hh = num / jnp.maximum(jnp.abs(den), jnp.exp(-mt[i]))
        mu = jnp.mean(hh, axis=-1, keepdims=True)
        d = hh - mu
        var = jnp.mean(d * d, axis=-1, keepdims=True)
        hn = d * lax.rsqrt(var + LN_EPS) * gain_ref[:, hs[i]]
        h_ref[b, :, hs[i]] = (_sigmoid(mo_ref[b, :, hs[i]]) * hn).astype(BF16)


def _mlstm(qkv, mo, g, gt, gain, batch, seq):
    lc = min(ML_CHUNK, seq)
    nc = seq // lc
    m = batch * seq
    row = lambda width, j: pl.BlockSpec((batch, lc, width), lambda c, j=j: (0, c, j))
    out = pl.pallas_call(
        functools.partial(_mlstm_kernel, lc=lc, nb=batch),
        grid=(nc,),
        in_specs=[row(HALF, 0), row(HALF, 1), row(HALF, 2), row(HALF, 3), row(HALF, 0), row(LANES, 0)]
        + [pl.BlockSpec((EV_GATES, lc), lambda c, b=b: (0, b * nc + c)) for b in range(batch)]
        + [pl.BlockSpec((1, HALF), lambda c: (0, 0)),
           pl.BlockSpec((BIAS_PARTS, LANES, FOX_PAIRS * LANES), lambda c: (0, 0, 0))],
        out_specs=[row(HALF, 0), row(FOX_PAIRS * KEXT_W, 0)],
        out_shape=[jax.ShapeDtypeStruct((batch, seq, HALF), BF16),
                   jax.ShapeDtypeStruct((batch, seq, FOX_PAIRS * KEXT_W), BF16)],
        scratch_shapes=[pltpu.VMEM((batch * ML_HEADS, ML_HD, 2 * ML_HD), F32),
                        pltpu.VMEM((batch, SUBLANES, LANES), F32),
                        pltpu.VMEM((batch, SUBLANES, LANES), F32)],
        compiler_params=_params("arbitrary"),
        name="mlstm",
    )(*([qkv.reshape(batch, seq, -1)] * 4), mo.reshape(batch, seq, HALF), g.reshape(batch, seq, LANES),
      *([gt] * batch), gain, _bias_routing())
    return out[0].reshape(m, HALF), out[1].reshape(m, FOX_PAIRS * KEXT_W)


BF16_ROWS = 2 * SUBLANES
FOX_ACC_ROWS = FOX_HD + BF16_ROWS


def _fox_kernel(qt_ref, kext_ref, vt_ref, o_ref, m_sc, acc_sc, s0_sc, s1_sc, mx0_sc, mx1_sc, *, blk, qw):
    ratio = qw // blk
    qi = pl.program_id(2)
    q2 = qt_ref[...]
    row = lax.broadcasted_iota(jnp.int32, (LANES, qw), 0)
    top = row < FOX_HD
    zero = jnp.zeros_like(q2)
    qe = []
    for hd in range(2):
        pick = (row >= BIAS_PARTS * hd) & (row < BIAS_PARTS * (hd + 1))
        minus1 = jnp.where(pick, -1.0, 0.0).astype(BF16)
        qh = jnp.where(top, q2, zero) if hd == 0 else jnp.where(top, zero, q2)
        qe.append(jnp.concatenate([qh, minus1], axis=0))

    m_sc[...] = jnp.full_like(m_sc, NEG)
    acc_sc[...] = jnp.zeros_like(acc_sc)
    s_bufs = (s0_sc, s1_sc)
    mx_bufs = (mx0_sc, mx1_sc)
    ones_rows = jnp.where(lax.broadcasted_iota(jnp.int32, (BF16_ROWS, blk), 0) == 0, 1.0, 0.0).astype(BF16)

    def scores(kj, slot, diag):
        k0 = pl.multiple_of(kj * blk, blk)
        ke = kext_ref[pl.ds(k0, blk), :]
        for hd in range(2):
            s = _dot(ke, qe[hd])
            if diag is not None:
                key = lax.broadcasted_iota(jnp.int32, (blk, qw), 0) + diag * blk
                s = jnp.where(key <= lax.broadcasted_iota(jnp.int32, (blk, qw), 1), s, NEG)
            s_bufs[slot][hd] = s
            mx_bufs[slot][hd] = jnp.max(s, axis=0, keepdims=True)

    def accumulate(kj, slot):
        k0 = pl.multiple_of(kj * blk, blk)
        for hd in range(2):
            rows = slice(hd * FOX_HD, (hd + 1) * FOX_HD)
            m_prev = m_sc[hd]
            m_next = jnp.maximum(m_prev, mx_bufs[slot][hd])
            alpha = jnp.exp2(m_prev - m_next)
            p = jnp.exp2(s_bufs[slot][hd] - m_next).astype(BF16)
            m_sc[hd] = m_next
            vt1 = jnp.concatenate([vt_ref[rows, pl.ds(k0, blk)], ones_rows], axis=0)
            acc_sc[hd] = alpha * acc_sc[hd] + _dot(vt1, p)

    n_off = ratio * qi

    def key_block(t):
        return jnp.where(t < ratio, ratio * qi + t, t - ratio)

    def step(t, slot, diag=None):
        scores(key_block(t), slot, diag)
        accumulate(key_block(t - 1), 1 - slot)

    scores(ratio * qi, 0, 0)
    for r in range(1, ratio):
        step(r, r % 2, r)

    def body(i, carry):
        for u in range(FOX_UNROLL):
            step(ratio + FOX_UNROLL * i + u, (ratio + u) % 2)
        return carry

    lax.fori_loop(0, n_off // FOX_UNROLL, body, 0)
    done = (n_off // FOX_UNROLL) * FOX_UNROLL
    width = FOX_UNROLL // 2
    while width >= 1:
        if width % math.gcd(ratio, FOX_UNROLL) == 0:
            @pl.when(n_off & width != 0)
            def _(done=done, width=width):
                for u in range(width):
                    step(ratio + done + u, (ratio + u) % 2)
            done = done + (n_off & width)
        width //= 2

    last = ratio + n_off - 1
    if ratio % 2 == 0:
        accumulate(key_block(last), (ratio - 1) % 2)
    else:
        for parity in range(2):
            @pl.when(last % 2 == parity)
            def _(parity=parity):
                accumulate(key_block(last), parity)

    out = [acc_sc[hd, 0:FOX_HD, :] * (1.0 / acc_sc[hd, FOX_HD:FOX_HD + 1, :]) for hd in range(2)]
    o_ref[...] = jnp.concatenate(out, axis=0).T.astype(BF16)


def _fox(qvt, kext, batch, seq):
    blk = min(FOX_BLOCK, seq)
    qw = min(FOX_QUERIES, seq)
    nq = seq // qw
    m = batch * seq
    return pl.pallas_call(
        functools.partial(_fox_kernel, blk=blk, qw=qw),
        grid=(batch, FOX_PAIRS, nq),
        in_specs=[pl.BlockSpec((LANES, qw), lambda b, p, i: (p, b * nq + i)),
                  pl.BlockSpec((seq, KEXT_W), lambda b, p, i: (b, p)),
                  pl.BlockSpec((LANES, seq), lambda b, p, i: (FOX_PAIRS + p, b))],
        out_specs=pl.BlockSpec((qw, LANES), lambda b, p, i: (b * nq + i, p)),
        out_shape=jax.ShapeDtypeStruct((m, HALF), BF16),
        scratch_shapes=[pltpu.VMEM((2, 1, qw), F32),
                        pltpu.VMEM((2, FOX_ACC_ROWS, qw), F32),
                        pltpu.VMEM((2, blk, qw), F32),
                        pltpu.VMEM((2, blk, qw), F32),
                        pltpu.VMEM((2, 1, qw), F32),
                        pltpu.VMEM((2, 1, qw), F32)],
        compiler_params=_params("arbitrary", "arbitrary", "arbitrary"),
        name="fox_attention",
    )(qvt, kext, qvt)


def _ffn_kernel(a_ref, b_ref, xin_ref, wo_ref, g1_ref, beta1_ref, wu_ref, cw_ref, cb_ref, wd_ref, g_ref, beta_ref,
                o_ref, he_ref, hprev_ref, act_ref, x_ref, *, tm, tiles_per_seq):
    i = pl.program_id(0)

    @pl.when(i % tiles_per_seq == 0)
    def _():
        hprev_ref[...] = jnp.zeros_like(hprev_ref)

    for r in range(EPILOGUE_SPLIT):
        rows = slice(r * tm // EPILOGUE_SPLIT, (r + 1) * tm // EPILOGUE_SPLIT)
        ab = jnp.concatenate([a_ref[rows, :], b_ref[rows, :]], axis=1)
        x_ref[rows, :] = _layer_norm(ALPHA * xin_ref[rows, :] + _dot(ab, wo_ref[...]), g1_ref[...], beta1_ref[...])

    xb = x_ref[...].astype(BF16)
    cw = FFN_COLS

    def conv_cols(cs):
        he_ref[0:SUBLANES, :] = hprev_ref[:, cs]
        he_ref[SUBLANES:, :] = _dot(xb, wu_ref[:, cs])
        hprev_ref[:, cs] = he_ref[tm:tm + SUBLANES, :]
        he = he_ref[...]
        conv = cb_ref[:, cs] + cw_ref[FFN_CONV - 1:FFN_CONV, cs] * he[SUBLANES:, :]
        for back in range(1, FFN_CONV):
            conv = conv + cw_ref[FFN_CONV - 1 - back:FFN_CONV - back, cs] * pltpu.roll(he, back, 0)[SUBLANES:, :]
        return conv

    for j in range(D_FF // cw):
        val = conv_cols(slice(j * cw, (j + 1) * cw))
        gate = conv_cols(slice(D_FF + j * cw, D_FF + (j + 1) * cw))
        act_ref[:, j * cw:(j + 1) * cw] = (_silu(gate) * val).astype(BF16)
    for r in range(EPILOGUE_SPLIT):
        rows = slice(r * tm // EPILOGUE_SPLIT, (r + 1) * tm // EPILOGUE_SPLIT)
        f = _dot(act_ref[rows, :], wd_ref[...])
        o_ref[rows, :] = _layer_norm(ALPHA * x_ref[rows, :] + f, g_ref[...], beta_ref[...])


def _outproj_ffn(a, b, x, w_out, ln1_g, ln1_b, w_up, conv_w, conv_b, w_down, ln2_g, ln2_b, seq):
    m = x.shape[0]
    tm = min(ROW_TILE, seq)
    row = lambda width: pl.BlockSpec((tm, width), lambda i: (i, 0))
    full = lambda shape: pl.BlockSpec(shape, lambda i: (0, 0), pipeline_mode=pl.Buffered(1))
    return pl.pallas_call(
        functools.partial(_ffn_kernel, tm=tm, tiles_per_seq=seq // tm),
        grid=(m // tm,),
        in_specs=[row(HALF), row(HALF), row(D_MODEL),
                  full((D_MODEL, D_MODEL)), full((1, D_MODEL)), full((1, D_MODEL)),
                  full((D_MODEL, 2 * D_FF)), full((FFN_CONV, 2 * D_FF)), full((1, 2 * D_FF)),
                  full((D_FF, D_MODEL)), full((1, D_MODEL)), full((1, D_MODEL))],
        out_specs=row(D_MODEL),
        out_shape=jax.ShapeDtypeStruct((m, D_MODEL), F32),
        scratch_shapes=[pltpu.VMEM((tm + SUBLANES, FFN_COLS), F32),
                        pltpu.VMEM((SUBLANES, 2 * D_FF), F32),
                        pltpu.VMEM((tm, D_FF), BF16),
                        pltpu.VMEM((tm, D_MODEL), F32)],
        compiler_params=_params("arbitrary"),
        name="outproj_conv_ffn",
    )(a, b, x, w_out.astype(BF16), ln1_g[None, :], ln1_b[None, :],
      w_up.astype(BF16), conv_w, conv_b[None, :], w_down.astype(BF16), ln2_g[None, :], ln2_b[None, :])


OD_MAIN = SSM_CONV_CH + 2 * HALF


def _inproj_odd_kernel(x_ref, wm_ref, wd_ref, wdt_ref, u_ref, dt_ref, dtt_ref):
    xb = x_ref[...].astype(BF16)
    for j in range(OD_MAIN // HALF):
        sl = slice(j * HALF, (j + 1) * HALF)
        u_ref[:, sl] = _dot(xb, wm_ref[:, sl])
    dt_ref[...] = _dot(xb, wd_ref[...])
    dtt_ref[...] = _dot_nt(wdt_ref[...], xb)


def _inproj_odd(x, w_in):
    m = x.shape[0]
    tm = ROW_TILE
    z1 = HALF
    x1 = z1 + SSM_CONV_CH
    d1 = x1 + SSM_HEADS
    wm = jnp.concatenate([w_in[:, z1:x1], w_in[:, :z1], w_in[:, d1:]], axis=1).astype(BF16)
    wd8 = w_in[:, x1:d1]
    wd = jnp.pad(wd8, ((0, 0), (0, LANES - SSM_HEADS))).astype(BF16)
    wdt = wd8.T.astype(BF16)
    full = lambda shape: pl.BlockSpec(shape, lambda i: (0, 0))
    return pl.pallas_call(
        _inproj_odd_kernel,
        grid=(m // tm,),
        in_specs=[pl.BlockSpec((tm, D_MODEL), lambda i: (i, 0)),
                  full((D_MODEL, OD_MAIN)), full((D_MODEL, LANES)), full((SSM_HEADS, D_MODEL))],
        out_specs=[pl.BlockSpec((tm, OD_MAIN), lambda i: (i, 0)),
                   pl.BlockSpec((tm, LANES), lambda i: (i, 0)),
                   pl.BlockSpec((SSM_HEADS, tm), lambda i: (0, i))],
        out_shape=[jax.ShapeDtypeStruct((m, OD_MAIN), F32),
                   jax.ShapeDtypeStruct((m, LANES), F32),
                   jax.ShapeDtypeStruct((SSM_HEADS, m), F32)],
        compiler_params=_params("arbitrary"),
        name="inproj_odd",
    )(x, wm, wd, wdt)


SSM_HPG = SSM_HEADS // SSM_GROUPS
GROUP_W = SSM_HPG * SSM_HEADDIM


def _ssd_kernel(xbc_ref, z_ref, dt_ref, dtt_ref, cw_ref, cb_ref, dtb_ref, dtbt_ref, a_ref, at_ref,
                dskip_ref, gain_ref, y_ref, xe_ref, ht_ref, *, lc):
    c = pl.program_id(1)

    @pl.when(c == 0)
    def _():
        xe_ref[0:SUBLANES, :] = jnp.zeros((SUBLANES, SSM_CONV_CH), F32)
        ht_ref[...] = jnp.zeros_like(ht_ref)

    xe_ref[SUBLANES:, :] = xbc_ref[...]
    xe = xe_ref[...]
    conv = cb_ref[...] + cw_ref[SSM_CONV - 1:SSM_CONV, :] * xe[SUBLANES:, :]
    for back in range(1, SSM_CONV):
        conv = conv + cw_ref[SSM_CONV - 1 - back:SSM_CONV - back, :] * pltpu.roll(xe, back, 0)[SUBLANES:, :]
    xe_ref[0:SUBLANES, :] = xe_ref[lc:lc + SUBLANES, :]
    xact = _silu(conv)
    xs = xact[:, :HALF]

    tril = _tri(lc, True)
    tril_b = jnp.where(tril, 1.0, 0.0).astype(BF16)
    triu_b = jnp.where(_tri(lc, False), 1.0, 0.0).astype(BF16)
    er = lax.broadcasted_iota(jnp.int32, (LANES, HALF), 0)
    ec = lax.broadcasted_iota(jnp.int32, (LANES, HALF), 1)
    head_of = jnp.right_shift(ec, SSM_HEADDIM.bit_length() - 1)
    expand = jnp.where(head_of == er, 1.0, 0.0).astype(BF16)

    dt_col = _softplus(dt_ref[...] + dtb_ref[...])
    dt_row = _softplus(dtt_ref[...] + dtbt_ref[...])
    acs_col = _dot_exact_rhs(tril_b, dt_col * a_ref[...])
    acs_row = _dot_exact_lhs(dt_row * at_ref[...], triu_b)
    acs_x = _dot_exact_lhs(acs_col, expand)
    dt_x = _dot_exact_lhs(dt_col, expand)
    xdt = xs * dt_x
    left = lax.broadcasted_iota(jnp.int32, (lc, LANES), 1) < SSM_HEADDIM

    groups = range(SSM_GROUPS)
    gs = [slice(gi * GROUP_W, (gi + 1) * GROUP_W) for gi in groups]
    bm = [xact[:, HALF + gi * SSM_STATE:HALF + (gi + 1) * SSM_STATE] for gi in groups]
    cmb = [xact[:, HALF + (SSM_GROUPS + gi) * SSM_STATE:HALF + (SSM_GROUPS + gi + 1) * SSM_STATE].astype(BF16)
           for gi in groups]
    ht = [ht_ref[gi] for gi in groups]
    cbm = [_dot_nt(cmb[gi], bm[gi].astype(BF16)) for gi in groups]
    off = [_dot(cmb[gi], ht[gi].astype(BF16)) for gi in groups]
    for gi in groups:
        acs_g = acs_x[:, gs[gi]]
        last = acs_g[lc - 1:lc, :]
        xw = (xdt[:, gs[gi]] * jnp.exp(last - acs_g)).astype(BF16)
        ht_ref[gi] = jnp.exp(last) * ht[gi] + _dot(bm[gi].T.astype(BF16), xw)
    ys = []
    for gi in groups:
        pair_out = []
        for pj in range(SSM_HPG // 2):
            lo = gi * GROUP_W + pj * LANES
            xpair = xdt[:, lo:lo + LANES].astype(BF16)
            yh = []
            for e in (gi * SSM_HPG + 2 * pj, gi * SSM_HPG + 2 * pj + 1):
                seg = jnp.where(tril, acs_col[:, e:e + 1] - acs_row[e:e + 1, :], NEG)
                yh.append(_dot((cbm[gi] * jnp.exp(seg)).astype(BF16), xpair))
            pair_out.append(jnp.where(left, yh[0], yh[1]))
        y_diag = jnp.concatenate(pair_out, axis=1)
        ys.append(y_diag + off[gi] * jnp.exp(acs_x[:, gs[gi]]))
    y = jnp.concatenate(ys, axis=1) + dskip_ref[...] * xs
    gated = y * _silu(z_ref[...])
    ms = jnp.mean(gated * gated, axis=-1, keepdims=True)
    y_ref[...] = (gated * lax.rsqrt(ms + LN_EPS) * gain_ref[...]).astype(BF16)


def _ssd(u, dt, dtt, conv_w, conv_b, dt_bias, a_log, d_skip, ssm_norm, batch, seq):
    lc = min(SSD_CHUNK, seq)
    nc = seq // lc
    m = batch * seq
    a = -jnp.exp(a_log.astype(F32))
    pad = lambda v: jnp.pad(v, (0, LANES - SSM_HEADS))[None, :]
    full = lambda shape: pl.BlockSpec(shape, lambda b, c: (0, 0))
    return pl.pallas_call(
        functools.partial(_ssd_kernel, lc=lc),
        grid=(batch, nc),
        in_specs=[pl.BlockSpec((lc, SSM_CONV_CH), lambda b, c: (b * nc + c, 0)),
                  pl.BlockSpec((lc, HALF), lambda b, c: (b * nc + c, SSM_CONV_CH // HALF)),
                  pl.BlockSpec((lc, LANES), lambda b, c: (b * nc + c, 0)),
                  pl.BlockSpec((SSM_HEADS, lc), lambda b, c: (0, b * nc + c)),
                  full((SSM_CONV, SSM_CONV_CH)), full((1, SSM_CONV_CH)),
                  full((1, LANES)), full((SSM_HEADS, 1)), full((1, LANES)), full((SSM_HEADS, 1)),
                  full((1, HALF)), full((1, HALF))],
        out_specs=pl.BlockSpec((lc, HALF), lambda b, c: (b * nc + c, 0)),
        out_shape=jax.ShapeDtypeStruct((m, HALF), BF16),
        scratch_shapes=[pltpu.VMEM((lc + SUBLANES, SSM_CONV_CH), F32),
                        pltpu.VMEM((SSM_GROUPS, SSM_STATE, GROUP_W), F32)],
        compiler_params=_params("arbitrary", "arbitrary"),
        name="ssd",
    )(u, u, dt, dtt, conv_w, conv_b[None, :], pad(dt_bias), dt_bias[:, None], pad(a), a[:, None],
      jnp.repeat(d_skip, SSM_HEADDIM)[None, :], ssm_norm[None, :])


POOL_HALO = 16


def _pool_kernel(u_ref, w_ref, b_ref, sc_ref, p_ref, ue_ref, *, tm, tiles_per_seq):
    i = pl.program_id(0)

    @pl.when(i % tiles_per_seq == 0)
    def _():
        ue_ref[0:POOL_HALO, :] = jnp.zeros((POOL_HALO, HALF), F32)

    ue_ref[POOL_HALO:, :] = u_ref[...]
    t = (i % tiles_per_seq) * tm + lax.broadcasted_iota(jnp.int32, (tm, 1), 0)
    for gi, win in enumerate(POOL_WINDOWS):
        gs = slice(gi * POOL_GW, (gi + 1) * POOL_GW)
        acc = ue_ref[:, gs]
        span = 1
        while span < win:
            acc = acc + pltpu.roll(acc, span, 0)
            span *= 2
        cnt = jnp.minimum(t + 1, win).astype(F32)
        pooled = acc[POOL_HALO:, :] / cnt - ue_ref[POOL_HALO:POOL_HALO + tm, gs]
        y = _dot(pooled.astype(BF16), w_ref[gi]) + b_ref[:, gs]
        p_ref[:, gs] = (y * sc_ref[:, gs]).astype(BF16)
    ue_ref[0:POOL_HALO, :] = ue_ref[tm:tm + POOL_HALO, :]


def _pool(u, pool_w, pool_b, pool_scale, seq):
    m = u.shape[0]
    tm = min(ROW_TILE, seq)
    full2 = lambda shape: pl.BlockSpec(shape, lambda i: (0, 0))
    return pl.pallas_call(
        functools.partial(_pool_kernel, tm=tm, tiles_per_seq=seq // tm),
        grid=(m // tm,),
        in_specs=[pl.BlockSpec((tm, HALF), lambda i: (i, (SSM_CONV_CH + HALF) // HALF)),
                  pl.BlockSpec((len(POOL_WINDOWS), POOL_GW, POOL_GW), lambda i: (0, 0, 0)),
                  full2((1, HALF)), full2((1, HALF))],
        out_specs=pl.BlockSpec((tm, HALF), lambda i: (i, 0)),
        out_shape=jax.ShapeDtypeStruct((m, HALF), BF16),
        scratch_shapes=[pltpu.VMEM((tm + POOL_HALO, HALF), F32)],
        compiler_params=_params("arbitrary"),
        name="pool",
    )(u, pool_w.astype(BF16), pool_b[None, :], pool_scale[None, :])


def kernel(x, ev_w_in, ev_b_in, ev_ml_norm, ev_w_out, od_w_in, od_conv_w, od_conv_b, od_dt_bias, od_a_log,
           od_d_skip, od_ssm_norm, od_pool_w, od_pool_b, od_pool_scale, od_w_out, ffn_w_up, ffn_conv_w,
           ffn_conv_b, ffn_w_down, ln1_g, ln1_b, ln2_g, ln2_b):
    batch, seq, _ = x.shape
    h = x.reshape(batch * seq, D_MODEL)
    for layer in range(DEPTH):
        j = layer // 2
        if layer % 2 == 0:
            qkv, mo, g, gt, qvt = _inproj_even(h, ev_w_in[j], ev_b_in[j])
            a, kext = _mlstm(qkv, mo, g, gt, ev_ml_norm[j][None, :], batch, seq)
            b = _fox(qvt, kext, batch, seq)
            w_out = ev_w_out[j]
        else:
            u, dt, dtt = _inproj_odd(h, od_w_in[j])
            a = _ssd(u, dt, dtt, od_conv_w[j], od_conv_b[j], od_dt_bias[j], od_a_log[j], od_d_skip[j],
                     od_ssm_norm[j], batch, seq)
            b = _pool(u, od_pool_w[j], od_pool_b[j], od_pool_scale[j], seq)
            w_out = od_w_out[j]
        h = _outproj_ffn(a, b, h, w_out, ln1_g[layer], ln1_b[layer], ffn_w_up[layer], ffn_conv_w[layer],
                         ffn_conv_b[layer], ffn_w_down[layer], ln2_g[layer], ln2_b[layer], seq)
    return h.reshape(batch, seq, D_MODEL)
```

```python
import functools
import math

import jax
import jax.numpy as jnp
import numpy as np
from jax import lax
from jax.experimental import pallas as pl
from jax.experimental.pallas import tpu as pltpu

F32 = jnp.float32
BF16 = jnp.bfloat16

D_MODEL = 1024
DEPTH = 4
HALF = D_MODEL // 2
ML_HEADS = 4
ML_HD = HALF // ML_HEADS
FOX_HEADS = 8
FOX_HD = HALF // FOX_HEADS
SSM_HEADDIM = 64
SSM_HEADS = HALF // SSM_HEADDIM
SSM_GROUPS = 2
SSM_STATE = 128
SSM_CONV = 4
SSM_CONV_CH = HALF + 2 * SSM_GROUPS * SSM_STATE
POOL_WINDOWS = (2, 4, 8, 16)
POOL_GW = HALF // len(POOL_WINDOWS)
D_FF = 256 * ((8 * D_MODEL // 3 + 255) // 256)
FFN_CONV = 3
LN_EPS = 1e-5
ALPHA = (2.0 * DEPTH) ** 0.25

LANES = 128
SUBLANES = 8
NEG = -1e30
LOG2E = 1.4426950408889634
VMEM_LIMIT = 56 * 1024 * 1024

ROW_TILE = 512
ML_CHUNK = 512
SSD_CHUNK = 256
FOX_BLOCK = 256
FOX_QUERIES = 512
FOX_UNROLL = 16
FFN_COLS = 256
EPILOGUE_SPLIT = 2


def _params(*sem):
    return pltpu.CompilerParams(dimension_semantics=sem, vmem_limit_bytes=VMEM_LIMIT)


def _dot(a, b):
    return jnp.dot(a, b, preferred_element_type=F32)


def _dot_nt(a, b):
    return lax.dot_general(a, b, (((1,), (1,)), ((), ())), preferred_element_type=F32)


def _split3(x):
    h1 = x.astype(BF16)
    r1 = x - h1.astype(F32)
    h2 = r1.astype(BF16)
    h3 = (r1 - h2.astype(F32)).astype(BF16)
    return h1, h2, h3


def _dot_exact_rhs(t, x):
    h1, h2, h3 = _split3(x)
    return _dot(t, h1) + _dot(t, h2) + _dot(t, h3)


def _dot_exact_lhs(x, t):
    h1, h2, h3 = _split3(x)
    return _dot(h1, t) + _dot(h2, t) + _dot(h3, t)


def _log_sigmoid(x):
    return jnp.minimum(x, 0.0) - jnp.log(1.0 + jnp.exp(-jnp.abs(x)))


def _softplus(x):
    return jnp.maximum(x, 0.0) + jnp.log(1.0 + jnp.exp(-jnp.abs(x)))


def _sigmoid(x):
    return 1.0 / (1.0 + jnp.exp(-x))


def _silu(x):
    return x * _sigmoid(x)


def _tri(n, lower):
    r = lax.broadcasted_iota(jnp.int32, (n, n), 0)
    c = lax.broadcasted_iota(jnp.int32, (n, n), 1)
    return (c <= r) if lower else (r <= c)


def _layer_norm(v, g, b):
    mu = jnp.mean(v, axis=-1, keepdims=True)
    d = v - mu
    var = jnp.mean(d * d, axis=-1, keepdims=True)
    return d * lax.rsqrt(var + LN_EPS) * g + b


EV_MAIN = 4 * HALF
EV_GATES = 2 * ML_HEADS + FOX_HEADS
FOX_QSCALE = FOX_HD ** -0.5 * LOG2E


def _inproj_even_kernel(x_ref, wm_ref, bm_ref, wo_ref, bo_ref, wg_ref, bg_ref, wgt_ref, bgt_ref,
                        wqvt_ref, bqvt_ref, qkv_ref, mo_ref, g_ref, gt_ref, qvt_ref):
    xb = x_ref[...].astype(BF16)
    for j in range(EV_MAIN // HALF):
        sl = slice(j * HALF, (j + 1) * HALF)
        qkv_ref[:, sl] = (_dot(xb, wm_ref[:, sl]) + bm_ref[:, sl]).astype(BF16)
    mo_ref[...] = _dot(xb, wo_ref[...]) + bo_ref[...]
    g_ref[...] = _dot(xb, wg_ref[...]) + bg_ref[...]
    gt_ref[...] = _dot_nt(wgt_ref[...], xb) + bgt_ref[...]
    fqt = (_dot_nt(wqvt_ref[0:HALF, :], xb) + bqvt_ref[0:HALF, :]) * FOX_QSCALE
    qvt_ref[0:HALF, :] = fqt.astype(BF16)
    qvt_ref[HALF:, :] = (_dot_nt(wqvt_ref[HALF:, :], xb) + bqvt_ref[HALF:, :]).astype(BF16)


def _inproj_even(x, w_in, b_in):
    m = x.shape[0]
    tm = ROW_TILE
    o = [0, HALF, 2 * HALF, 3 * HALF, 4 * HALF, 4 * HALF + ML_HEADS, 4 * HALF + 2 * ML_HEADS]
    fq0 = o[6]
    fk0 = fq0 + HALF
    fv0 = fq0 + 2 * HALF
    ff0 = fq0 + 3 * HALF
    wm = jnp.concatenate([w_in[:, :3 * HALF], w_in[:, fk0:fv0]], axis=1).astype(BF16)
    bm = jnp.concatenate([b_in[:3 * HALF], b_in[fk0:fv0]])[None, :]
    wvt = jnp.concatenate([w_in[:, fq0:fk0], w_in[:, fv0:ff0]], axis=1).T.astype(BF16)
    bvt = jnp.concatenate([b_in[fq0:fk0], b_in[fv0:ff0]])[:, None]
    wo = w_in[:, 3 * HALF:4 * HALF].astype(BF16)
    bo = b_in[3 * HALF:4 * HALF][None, :]
    wg16 = jnp.concatenate([w_in[:, o[4]:o[6]], w_in[:, ff0:]], axis=1)
    bg16 = jnp.concatenate([b_in[o[4]:o[6]], b_in[ff0:]])
    wg = jnp.pad(wg16, ((0, 0), (0, LANES - EV_GATES))).astype(BF16)
    bg = jnp.pad(bg16, (0, LANES - EV_GATES))[None, :]
    wgt = wg16.T.astype(BF16)
    bgt = bg16[:, None]
    full = lambda shape: pl.BlockSpec(shape, lambda i: (0, 0))
    return pl.pallas_call(
        _inproj_even_kernel,
        grid=(m // tm,),
        in_specs=[pl.BlockSpec((tm, D_MODEL), lambda i: (i, 0)),
                  full((D_MODEL, EV_MAIN)), full((1, EV_MAIN)),
                  full((D_MODEL, HALF)), full((1, HALF)),
                  full((D_MODEL, LANES)), full((1, LANES)),
                  full((EV_GATES, D_MODEL)), full((EV_GATES, 1)),
                  full((2 * HALF, D_MODEL)), full((2 * HALF, 1))],
        out_specs=[pl.BlockSpec((tm, EV_MAIN), lambda i: (i, 0)),
                   pl.BlockSpec((tm, HALF), lambda i: (i, 0)),
                   pl.BlockSpec((tm, LANES), lambda i: (i, 0)),
                   pl.BlockSpec((EV_GATES, tm), lambda i: (0, i)),
                   pl.BlockSpec((2 * HALF, tm), lambda i: (0, i))],
        out_shape=[jax.ShapeDtypeStruct((m, EV_MAIN), BF16),
                   jax.ShapeDtypeStruct((m, HALF), F32),
                   jax.ShapeDtypeStruct((m, LANES), F32),
                   jax.ShapeDtypeStruct((EV_GATES, m), F32),
                   jax.ShapeDtypeStruct((2 * HALF, m), BF16)],
        compiler_params=_params("arbitrary"),
        name="inproj_even",
    )(x, wm, bm, wo, bo, wg, bg, wgt, bgt, wvt, bvt)


ML_LOG_SCALE = math.log(ML_HD ** -0.5)


FOX_PAIRS = FOX_HEADS // 2
KEXT_W = 2 * LANES
BIAS_PARTS = 3


def _bias_routing():
    r = np.zeros((BIAS_PARTS, LANES, FOX_PAIRS * LANES), np.float32)
    for head in range(FOX_HEADS):
        for part in range(BIAS_PARTS):
            r[part, 2 * ML_HEADS + head, (head // 2) * LANES + BIAS_PARTS * (head % 2) + part] = 1.0
    return jnp.asarray(r, BF16)


def _mlstm_kernel(*refs, lc, nb):
    q_ref, k_ref, v_ref, fk_ref, mo_ref, g_ref = refs[:6]
    gt_refs = refs[6:6 + nb]
    gain_ref, route_ref, h_ref, kext_ref, ct_ref, m_ref, fcar_ref = refs[6 + nb:]
    c = pl.program_id(0)

    @pl.when(c == 0)
    def _():
        ct_ref[...] = jnp.zeros_like(ct_ref)
        m_ref[...] = jnp.zeros_like(m_ref)
        fcar_ref[...] = jnp.zeros_like(fcar_ref)

    tril = _tri(lc, True)
    tril_b = jnp.where(tril, 1.0, 0.0).astype(BF16)
    triu_b = jnp.where(_tri(lc, False), 1.0, 0.0).astype(BF16)
    batches = range(nb)
    g = [g_ref[b] for b in batches]
    gt = [gt_refs[b][...] for b in batches]
    bc_col = [_dot_exact_rhs(tril_b, _log_sigmoid(g[b])) for b in batches]
    bc_row = [_dot_exact_lhs(_log_sigmoid(gt[b]), triu_b) for b in batches]

    for b in batches:
        f_blk = bc_col[b] + fcar_ref[b, 0:1, :]
        fcar_ref[b] = jnp.broadcast_to(f_blk[lc - 1:lc, :], fcar_ref.shape[1:])
        parts = _split3(f_blk * LOG2E)
        bias = sum(_dot(parts[i], route_ref[i]) for i in range(BIAS_PARTS)).astype(BF16)
        for p in range(FOX_PAIRS):
            kext_ref[b, :, p * KEXT_W:p * KEXT_W + LANES] = fk_ref[b, :, p * LANES:(p + 1) * LANES]
            kext_ref[b, :, p * KEXT_W + LANES:(p + 1) * KEXT_W] = bias[:, p * LANES:(p + 1) * LANES]

    lane = lax.broadcasted_iota(jnp.int32, (lc, ML_HD), 1)
    ones_col = jnp.where(lane == 0, 1.0, 0.0).astype(BF16)
    streams = [(b, h) for b in batches for h in range(ML_HEADS)]
    ids = range(len(streams))
    hs = [slice(h * ML_HD, (h + 1) * ML_HD) for _, h in streams]
    vext = [jnp.concatenate([v_ref[b, :, hs[i]], ones_col], axis=1) for i, (b, _) in enumerate(streams)]
    bcl = [bc_col[b][:, ML_HEADS + h:ML_HEADS + h + 1] for b, h in streams]
    bcr = [bc_row[b][ML_HEADS + h:ML_HEADS + h + 1, :] for b, h in streams]
    iil = [g[b][:, h:h + 1] for b, h in streams]
    iir = [gt[b][h:h + 1, :] for b, h in streams]
    m_st = [m_ref[b, h:h + 1, 0:1] for b, h in streams]
    ct = [ct_ref[i] for i in ids]

    s = [_dot_nt(q_ref[b, :, hs[i]], k_ref[b, :, hs[i]]) for i, (b, _) in enumerate(streams)]
    inter = [_dot(q_ref[b, :, hs[i]], ct[i].astype(BF16)) for i, (b, _) in enumerate(streams)]

    for i, (b, h) in enumerate(streams):
        btot = bcr[i][:, lc - 1:lc]
        m_new = jnp.maximum(btot + m_st[i], jnp.max(btot - bcr[i] + iir[i], axis=1, keepdims=True))
        decay = jnp.exp(btot + m_st[i] - m_new)
        ws = jnp.exp(btot - bcl[i] + iil[i] - m_new)
        kwt = (k_ref[b, :, hs[i]].astype(F32) * ws).T.astype(BF16)
        ct_ref[i] = decay * ct[i] + _dot(kwt, vext[i])
        m_ref[b, h:h + 1, :] = jnp.broadcast_to(m_new, (1, LANES))

    mt, w_inter, intra = [], [], []
    for i in ids:
        a_col = bcl[i] + m_st[i]
        dmat = jnp.where(tril, bcl[i] + (iir[i] - bcr[i]), NEG)
        mt.append(jnp.maximum(a_col, jnp.max(dmat, axis=1, keepdims=True)))
        mts = mt[i] - ML_LOG_SCALE
        w_inter.append(jnp.exp(a_col - mts))
        pw = (s[i] * jnp.exp(dmat - mts)).astype(BF16)
        intra.append(_dot(pw, vext[i]))

    for i, (b, _) in enumerate(streams):
        tot = intra[i] + w_inter[i] * inter[i]
        num = tot[:, :ML_HD]
        den = tot[:, ML_HD:ML_HD + 1]
        hh = num / jnp.maximum(jnp.abs(den), jnp.exp(-mt[i]))
        mu = jnp.mean(hh, axis=-1, keepdims=True)
        d = hh - mu
        var = jnp.mean(d * d, axis=-1, keepdims=True)
        hn = d * lax.rsqrt(var + LN_EPS) * gain_ref[:, hs[i]]
        h_ref[b, :, hs[i]] = (_sigmoid(mo_ref[b, :, hs[i]]) * hn).astype(BF16)


def _mlstm(qkv, mo, g, gt, gain, batch, seq):
    lc = min(ML_CHUNK, seq)
    nc = seq // lc
    m = batch * seq
    row = lambda width, j: pl.BlockSpec((batch, lc, width), lambda c, j=j: (0, c, j))
    out = pl.pallas_call(
        functools.partial(_mlstm_kernel, lc=lc, nb=batch),
        grid=(nc,),
        in_specs=[row(HALF, 0), row(HALF, 1), row(HALF, 2), row(HALF, 3), row(HALF, 0), row(LANES, 0)]
        + [pl.BlockSpec((EV_GATES, lc), lambda c, b=b: (0, b * nc + c)) for b in range(batch)]
        + [pl.BlockSpec((1, HALF), lambda c: (0, 0)),
           pl.BlockSpec((BIAS_PARTS, LANES, FOX_PAIRS * LANES), lambda c: (0, 0, 0))],
        out_specs=[row(HALF, 0), row(FOX_PAIRS * KEXT_W, 0)],
        out_shape=[jax.ShapeDtypeStruct((batch, seq, HALF), BF16),
                   jax.ShapeDtypeStruct((batch, seq, FOX_PAIRS * KEXT_W), BF16)],
        scratch_shapes=[pltpu.VMEM((batch * ML_HEADS, ML_HD, 2 * ML_HD), F32),
                        pltpu.VMEM((batch, SUBLANES, LANES), F32),
                        pltpu.VMEM((batch, SUBLANES, LANES), F32)],
        compiler_params=_params("arbitrary"),
        name="mlstm",
    )(*([qkv.reshape(batch, seq, -1)] * 4), mo.reshape(batch, seq, HALF), g.reshape(batch, seq, LANES),
      *([gt] * batch), gain, _bias_routing())
    return out[0].reshape(m, HALF), out[1].reshape(m, FOX_PAIRS * KEXT_W)


BF16_ROWS = 2 * SUBLANES
FOX_ACC_ROWS = FOX_HD + BF16_ROWS


def _fox_kernel(qt_ref, kext_ref, vt_ref, o_ref, m_sc, acc_sc, s0_sc, s1_sc, mx0_sc, mx1_sc, *, blk, qw):
    ratio = qw // blk
    qi = pl.program_id(2)
    q2 = qt_ref[...]
    row = lax.broadcasted_iota(jnp.int32, (LANES, qw), 0)
    top = row < FOX_HD
    zero = jnp.zeros_like(q2)
    qe = []
    for hd in range(2):
        pick = (row >= BIAS_PARTS * hd) & (row < BIAS_PARTS * (hd + 1))
        minus1 = jnp.where(pick, -1.0, 0.0).astype(BF16)
        qh = jnp.where(top, q2, zero) if hd == 0 else jnp.where(top, zero, q2)
        qe.append(jnp.concatenate([qh, minus1], axis=0))

    m_sc[...] = jnp.full_like(m_sc, NEG)
    acc_sc[...] = jnp.zeros_like(acc_sc)
    s_bufs = (s0_sc, s1_sc)
    mx_bufs = (mx0_sc, mx1_sc)
    ones_rows = jnp.where(lax.broadcasted_iota(jnp.int32, (BF16_ROWS, blk), 0) == 0, 1.0, 0.0).astype(BF16)

    def scores(kj, slot, diag):
        k0 = pl.multiple_of(kj * blk, blk)
        ke = kext_ref[pl.ds(k0, blk), :]
        for hd in range(2):
            s = _dot(ke, qe[hd])
            if diag is not None:
                key = lax.broadcasted_iota(jnp.int32, (blk, qw), 0) + diag * blk
                s = jnp.where(key <= lax.broadcasted_iota(jnp.int32, (blk, qw), 1), s, NEG)
            s_bufs[slot][hd] = s
            mx_bufs[slot][hd] = jnp.max(s, axis=0, keepdims=True)

    def accumulate(kj, slot):
        k0 = pl.multiple_of(kj * blk, blk)
        for hd in range(2):
            rows = slice(hd * FOX_HD, (hd + 1) * FOX_HD)
            m_prev = m_sc[hd]
            m_next = jnp.maximum(m_prev, mx_bufs[slot][hd])
            alpha = jnp.exp2(m_prev - m_next)
            p = jnp.exp2(s_bufs[slot][hd] - m_next).astype(BF16)
            m_sc[hd] = m_next
            vt1 = jnp.concatenate([vt_ref[rows, pl.ds(k0, blk)], ones_rows], axis=0)
            acc_sc[hd] = alpha * acc_sc[hd] + _dot(vt1, p)

    n_off = ratio * qi

    def key_block(t):
        return jnp.where(t < ratio, ratio * qi + t, t - ratio)

    def step(t, slot, diag=None):
        scores(key_block(t), slot, diag)
        accumulate(key_block(t - 1), 1 - slot)

    scores(ratio * qi, 0, 0)
    for r in range(1, ratio):
        step(r, r % 2, r)

    def body(i, carry):
        for u in range(FOX_UNROLL):
            step(ratio + FOX_UNROLL * i + u, (ratio + u) % 2)
        return carry

    lax.fori_loop(0, n_off // FOX_UNROLL, body, 0)
    done = (n_off // FOX_UNROLL) * FOX_UNROLL
    width = FOX_UNROLL // 2
    while width >= 1:
        if width % math.gcd(ratio, FOX_UNROLL) == 0:
            @pl.when(n_off & width != 0)
            def _(done=done, width=width):
                for u in range(width):
                    step(ratio + done + u, (ratio + u) % 2)
            done = done + (n_off & width)
        width //= 2

    last = ratio + n_off - 1
    if ratio % 2 == 0:
        accumulate(key_block(last), (ratio - 1) % 2)
    else:
        for parity in range(2):
            @pl.when(last % 2 == parity)
            def _(parity=parity):
                accumulate(key_block(last), parity)

    out = [acc_sc[hd, 0:FOX_HD, :] * (1.0 / acc_sc[hd, FOX_HD:FOX_HD + 1, :]) for hd in range(2)]
    o_ref[...] = jnp.concatenate(out, axis=0).T.astype(BF16)


def _fox(qvt, kext, batch, seq):
    blk = min(FOX_BLOCK, seq)
    qw = min(FOX_QUERIES, seq)
    nq = seq // qw
    m = batch * seq
    return pl.pallas_call(
        functools.partial(_fox_kernel, blk=blk, qw=qw),
        grid=(batch, FOX_PAIRS, nq),
        in_specs=[pl.BlockSpec((LANES, qw), lambda b, p, i: (p, b * nq + i)),
                  pl.BlockSpec((seq, KEXT_W), lambda b, p, i: (b, p)),
                  pl.BlockSpec((LANES, seq), lambda b, p, i: (FOX_PAIRS + p, b))],
        out_specs=pl.BlockSpec((qw, LANES), lambda b, p, i: (b * nq + i, p)),
        out_shape=jax.ShapeDtypeStruct((m, HALF), BF16),
        scratch_shapes=[pltpu.VMEM((2, 1, qw), F32),
                        pltpu.VMEM((2, FOX_ACC_ROWS, qw), F32),
                        pltpu.VMEM((2, blk, qw), F32),
                        pltpu.VMEM((2, blk, qw), F32),
                        pltpu.VMEM((2, 1, qw), F32),
                        pltpu.VMEM((2, 1, qw), F32)],
        compiler_params=_params("arbitrary", "arbitrary", "arbitrary"),
        name="fox_attention",
    )(qvt, kext, qvt)


def _ffn_kernel(a_ref, b_ref, xin_ref, wo_ref, g1_ref, beta1_ref, wu_ref, cw_ref, cb_ref, wd_ref, g_ref, beta_ref,
                o_ref, he_ref, hprev_ref, act_ref, x_ref, *, tm, tiles_per_seq):
    i = pl.program_id(0)

    @pl.when(i % tiles_per_seq == 0)
    def _():
        hprev_ref[...] = jnp.zeros_like(hprev_ref)

    for r in range(EPILOGUE_SPLIT):
        rows = slice(r * tm // EPILOGUE_SPLIT, (r + 1) * tm // EPILOGUE_SPLIT)
        ab = jnp.concatenate([a_ref[rows, :], b_ref[rows, :]], axis=1)
        x_ref[rows, :] = _layer_norm(ALPHA * xin_ref[rows, :] + _dot(ab, wo_ref[...]), g1_ref[...], beta1_ref[...])

    xb = x_ref[...].astype(BF16)
    cw = FFN_COLS

    def conv_cols(cs):
        he_ref[0:SUBLANES, :] = hprev_ref[:, cs]
        he_ref[SUBLANES:, :] = _dot(xb, wu_ref[:, cs])
        hprev_ref[:, cs] = he_ref[tm:tm + SUBLANES, :]
        he = he_ref[...]
        conv = cb_ref[:, cs] + cw_ref[FFN_CONV - 1:FFN_CONV, cs] * he[SUBLANES:, :]
        for back in range(1, FFN_CONV):
            conv = conv + cw_ref[FFN_CONV - 1 - back:FFN_CONV - back, cs] * pltpu.roll(he, back, 0)[SUBLANES:, :]
        return conv

    for j in range(D_FF // cw):
        val = conv_cols(slice(j * cw, (j + 1) * cw))
        gate = conv_cols(slice(D_FF + j * cw, D_FF + (j + 1) * cw))
        act_ref[:, j * cw:(j + 1) * cw] = (_silu(gate) * val).astype(BF16)
    for r in range(EPILOGUE_SPLIT):
        rows = slice(r * tm // EPILOGUE_SPLIT, (r + 1) * tm // EPILOGUE_SPLIT)
        f = _dot(act_ref[rows, :], wd_ref[...])
        o_ref[rows, :] = _layer_norm(ALPHA * x_ref[rows, :] + f, g_ref[...], beta_ref[...])


def _outproj_ffn(a, b, x, w_out, ln1_g, ln1_b, w_up, conv_w, conv_b, w_down, ln2_g, ln2_b, seq):
    m = x.shape[0]
    tm = min(ROW_TILE, seq)
    row = lambda width: pl.BlockSpec((tm, width), lambda i: (i, 0))
    full = lambda shape: pl.BlockSpec(shape, lambda i: (0, 0), pipeline_mode=pl.Buffered(1))
    return pl.pallas_call(
        functools.partial(_ffn_kernel, tm=tm, tiles_per_seq=seq // tm),
        grid=(m // tm,),
        in_specs=[row(HALF), row(HALF), row(D_MODEL),
                  full((D_MODEL, D_MODEL)), full((1, D_MODEL)), full((1, D_MODEL)),
                  full((D_MODEL, 2 * D_FF)), full((FFN_CONV, 2 * D_FF)), full((1, 2 * D_FF)),
                  full((D_FF, D_MODEL)), full((1, D_MODEL)), full((1, D_MODEL))],
        out_specs=row(D_MODEL),
        out_shape=jax.ShapeDtypeStruct((m, D_MODEL), F32),
        scratch_shapes=[pltpu.VMEM((tm + SUBLANES, FFN_COLS), F32),
                        pltpu.VMEM((SUBLANES, 2 * D_FF), F32),
                        pltpu.VMEM((tm, D_FF), BF16),
                        pltpu.VMEM((tm, D_MODEL), F32)],
        compiler_params=_params("arbitrary"),
        name="outproj_conv_ffn",
    )(a, b, x, w_out.astype(BF16), ln1_g[None, :], ln1_b[None, :],
      w_up.astype(BF16), conv_w, conv_b[None, :], w_down.astype(BF16), ln2_g[None, :], ln2_b[None, :])


OD_MAIN = SSM_CONV_CH + 2 * HALF


def _inproj_odd_kernel(x_ref, wm_ref, wd_ref, wdt_ref, u_ref, dt_ref, dtt_ref):
    xb = x_ref[...].astype(BF16)
    for j in range(OD_MAIN // HALF):
        sl = slice(j * HALF, (j + 1) * HALF)
        u_ref[:, sl] = _dot(xb, wm_ref[:, sl])
    dt_ref[...] = _dot(xb, wd_ref[...])
    dtt_ref[...] = _dot_nt(wdt_ref[...], xb)


def _inproj_odd(x, w_in):
    m = x.shape[0]
    tm = ROW_TILE
    z1 = HALF
    x1 = z1 + SSM_CONV_CH
    d1 = x1 + SSM_HEADS
    wm = jnp.concatenate([w_in[:, z1:x1], w_in[:, :z1], w_in[:, d1:]], axis=1).astype(BF16)
    wd8 = w_in[:, x1:d1]
    wd = jnp.pad(wd8, ((0, 0), (0, LANES - SSM_HEADS))).astype(BF16)
    wdt = wd8.T.astype(BF16)
    full = lambda shape: pl.BlockSpec(shape, lambda i: (0, 0))
    return pl.pallas_call(
        _inproj_odd_kernel,
        grid=(m // tm,),
        in_specs=[pl.BlockSpec((tm, D_MODEL), lambda i: (i, 0)),
                  full((D_MODEL, OD_MAIN)), full((D_MODEL, LANES)), full((SSM_HEADS, D_MODEL))],
        out_specs=[pl.BlockSpec((tm, OD_MAIN), lambda i: (i, 0)),
                   pl.BlockSpec((tm, LANES), lambda i: (i, 0)),
                   pl.BlockSpec((SSM_HEADS, tm), lambda i: (0, i))],
        out_shape=[jax.ShapeDtypeStruct((m, OD_MAIN), F32),
                   jax.ShapeDtypeStruct((m, LANES), F32),
                   jax.ShapeDtypeStruct((SSM_HEADS, m), F32)],
        compiler_params=_params("arbitrary"),
        name="inproj_odd",
    )(x, wm, wd, wdt)


SSM_HPG = SSM_HEADS // SSM_GROUPS
GROUP_W = SSM_HPG * SSM_HEADDIM


def _ssd_kernel(xbc_ref, z_ref, dt_ref, dtt_ref, cw_ref, cb_ref, dtb_ref, dtbt_ref, a_ref, at_ref,
                dskip_ref, gain_ref, y_ref, xe_ref, ht_ref, *, lc):
    c = pl.program_id(1)

    @pl.when(c == 0)
    def _():
        xe_ref[0:SUBLANES, :] = jnp.zeros((SUBLANES, SSM_CONV_CH), F32)
        ht_ref[...] = jnp.zeros_like(ht_ref)

    xe_ref[SUBLANES:, :] = xbc_ref[...]
    xe = xe_ref[...]
    conv = cb_ref[...] + cw_ref[SSM_CONV - 1:SSM_CONV, :] * xe[SUBLANES:, :]
    for back in range(1, SSM_CONV):
        conv = conv + cw_ref[SSM_CONV - 1 - back:SSM_CONV - back, :] * pltpu.roll(xe, back, 0)[SUBLANES:, :]
    xe_ref[0:SUBLANES, :] = xe_ref[lc:lc + SUBLANES, :]
    xact = _silu(conv)
    xs = xact[:, :HALF]

    tril = _tri(lc, True)
    tril_b = jnp.where(tril, 1.0, 0.0).astype(BF16)
    triu_b = jnp.where(_tri(lc, False), 1.0, 0.0).astype(BF16)
    er = lax.broadcasted_iota(jnp.int32, (LANES, HALF), 0)
    ec = lax.broadcasted_iota(jnp.int32, (LANES, HALF), 1)
    head_of = jnp.right_shift(ec, SSM_HEADDIM.bit_length() - 1)
    expand = jnp.where(head_of == er, 1.0, 0.0).astype(BF16)

    dt_col = _softplus(dt_ref[...] + dtb_ref[...])
    dt_row = _softplus(dtt_ref[...] + dtbt_ref[...])
    acs_col = _dot_exact_rhs(tril_b, dt_col * a_ref[...])
    acs_row = _dot_exact_lhs(dt_row * at_ref[...], triu_b)
    acs_x = _dot_exact_lhs(acs_col, expand)
    dt_x = _dot_exact_lhs(dt_col, expand)
    xdt = xs * dt_x
    left = lax.broadcasted_iota(jnp.int32, (lc, LANES), 1) < SSM_HEADDIM

    groups = range(SSM_GROUPS)
    gs = [slice(gi * GROUP_W, (gi + 1) * GROUP_W) for gi in groups]
    bm = [xact[:, HALF + gi * SSM_STATE:HALF + (gi + 1) * SSM_STATE] for gi in groups]
    cmb = [xact[:, HALF + (SSM_GROUPS + gi) * SSM_STATE:HALF + (SSM_GROUPS + gi + 1) * SSM_STATE].astype(BF16)
           for gi in groups]
    ht = [ht_ref[gi] for gi in groups]
    cbm = [_dot_nt(cmb[gi], bm[gi].astype(BF16)) for gi in groups]
    off = [_dot(cmb[gi], ht[gi].astype(BF16)) for gi in groups]
    for gi in groups:
        acs_g = acs_x[:, gs[gi]]
        last = acs_g[lc - 1:lc, :]
        xw = (xdt[:, gs[gi]] * jnp.exp(last - acs_g)).astype(BF16)
        ht_ref[gi] = jnp.exp(last) * ht[gi] + _dot(bm[gi].T.astype(BF16), xw)
    ys = []
    for gi in groups:
        pair_out = []
        for pj in range(SSM_HPG // 2):
            lo = gi * GROUP_W + pj * LANES
            xpair = xdt[:, lo:lo + LANES].astype(BF16)
            yh = []
            for e in (gi * SSM_HPG + 2 * pj, gi * SSM_HPG + 2 * pj + 1):
                seg = jnp.where(tril, acs_col[:, e:e + 1] - acs_row[e:e + 1, :], NEG)
                yh.append(_dot((cbm[gi] * jnp.exp(seg)).astype(BF16), xpair))
            pair_out.append(jnp.where(left, yh[0], yh[1]))
        y_diag = jnp.concatenate(pair_out, axis=1)
        ys.append(y_diag + off[gi] * jnp.exp(acs_x[:, gs[gi]]))
    y = jnp.concatenate(ys, axis=1) + dskip_ref[...] * xs
    gated = y * _silu(z_ref[...])
    ms = jnp.mean(gated * gated, axis=-1, keepdims=True)
    y_ref[...] = (gated * lax.rsqrt(ms + LN_EPS) * gain_ref[...]).astype(BF16)


def _ssd(u, dt, dtt, conv_w, conv_b, dt_bias, a_log, d_skip, ssm_norm, batch, seq):
    lc = min(SSD_CHUNK, seq)
    nc = seq // lc
    m = batch * seq
    a = -jnp.exp(a_log.astype(F32))
    pad = lambda v: jnp.pad(v, (0, LANES - SSM_HEADS))[None, :]
    full = lambda shape: pl.BlockSpec(shape, lambda b, c: (0, 0))
    return pl.pallas_call(
        functools.partial(_ssd_kernel, lc=lc),
        grid=(batch, nc),
        in_specs=[pl.BlockSpec((lc, SSM_CONV_CH), lambda b, c: (b * nc + c, 0)),
                  pl.BlockSpec((lc, HALF), lambda b, c: (b * nc + c, SSM_CONV_CH // HALF)),
                  pl.BlockSpec((lc, LANES), lambda b, c: (b * nc + c, 0)),
                  pl.BlockSpec((SSM_HEADS, lc), lambda b, c: (0, b * nc + c)),
                  full((SSM_CONV, SSM_CONV_CH)), full((1, SSM_CONV_CH)),
                  full((1, LANES)), full((SSM_HEADS, 1)), full((1, LANES)), full((SSM_HEADS, 1)),
                  full((1, HALF)), full((1, HALF))],
        out_specs=pl.BlockSpec((lc, HALF), lambda b, c: (b * nc + c, 0)),
        out_shape=jax.ShapeDtypeStruct((m, HALF), BF16),
        scratch_shapes=[pltpu.VMEM((lc + SUBLANES, SSM_CONV_CH), F32),
                        pltpu.VMEM((SSM_GROUPS, SSM_STATE, GROUP_W), F32)],
        compiler_params=_params("arbitrary", "arbitrary"),
        name="ssd",
    )(u, u, dt, dtt, conv_w, conv_b[None, :], pad(dt_bias), dt_bias[:, None], pad(a), a[:, None],
      jnp.repeat(d_skip, SSM_HEADDIM)[None, :], ssm_norm[None, :])


POOL_HALO = 16


def _pool_kernel(u_ref, w_ref, b_ref, sc_ref, p_ref, ue_ref, *, tm, tiles_per_seq):
    i = pl.program_id(0)

    @pl.when(i % tiles_per_seq == 0)
    def _():
        ue_ref[0:POOL_HALO, :] = jnp.zeros((POOL_HALO, HALF), F32)

    ue_ref[POOL_HALO:, :] = u_ref[...]
    t = (i % tiles_per_seq) * tm + lax.broadcasted_iota(jnp.int32, (tm, 1), 0)
    for gi, win in enumerate(POOL_WINDOWS):
        gs = slice(gi * POOL_GW, (gi + 1) * POOL_GW)
        acc = ue_ref[:, gs]
        span = 1
        while span < win:
            acc = acc + pltpu.roll(acc, span, 0)
            span *= 2
        cnt = jnp.minimum(t + 1, win).astype(F32)
        pooled = acc[POOL_HALO:, :] / cnt - ue_ref[POOL_HALO:POOL_HALO + tm, gs]
        y = _dot(pooled.astype(BF16), w_ref[gi]) + b_ref[:, gs]
        p_ref[:, gs] = (y * sc_ref[:, gs]).astype(BF16)
    ue_ref[0:POOL_HALO, :] = ue_ref[tm:tm + POOL_HALO, :]


def _pool(u, pool_w, pool_b, pool_scale, seq):
    m = u.shape[0]
    tm = min(ROW_TILE, seq)
    full2 = lambda shape: pl.BlockSpec(shape, lambda i: (0, 0))
    return pl.pallas_call(
        functools.partial(_pool_kernel, tm=tm, tiles_per_seq=seq // tm),
        grid=(m // tm,),
        in_specs=[pl.BlockSpec((tm, HALF), lambda i: (i, (SSM_CONV_CH + HALF) // HALF)),
                  pl.BlockSpec((len(POOL_WINDOWS), POOL_GW, POOL_GW), lambda i: (0, 0, 0)),
                  full2((1, HALF)), full2((1, HALF))],
        out_specs=pl.BlockSpec((tm, HALF), lambda i: (i, 0)),
        out_shape=jax.ShapeDtypeStruct((m, HALF), BF16),
        scratch_shapes=[pltpu.VMEM((tm + POOL_HALO, HALF), F32)],
        compiler_params=_params("arbitrary"),
        name="pool",
    )(u, pool_w.astype(BF16), pool_b[None, :], pool_scale[None, :])


def kernel(x, ev_w_in, ev_b_in, ev_ml_norm, ev_w_out, od_w_in, od_conv_w, od_conv_b, od_dt_bias, od_a_log,
           od_d_skip, od_ssm_norm, od_pool_w, od_pool_b, od_pool_scale, od_w_out, ffn_w_up, ffn_conv_w,
           ffn_conv_b, ffn_w_down, ln1_g, ln1_b, ln2_g, ln2_b):
    batch, seq, _ = x.shape
    h = x.reshape(batch * seq, D_MODEL)
    for layer in range(DEPTH):
        j = layer // 2
        if layer % 2 == 0:
            qkv, mo, g, gt, qvt = _inproj_even(h, ev_w_in[j], ev_b_in[j])
            a, kext = _mlstm(qkv, mo, g, gt, ev_ml_norm[j][None, :], batch, seq)
            b = _fox(qvt, kext, batch, seq)
            w_out = ev_w_out[j]
        else:
            u, dt, dtt = _inproj_odd(h, od_w_in[j])
            a = _ssd(u, dt, dtt, od_conv_w[j], od_conv_b[j], od_dt_bias[j], od_a_log[j], od_d_skip[j],
                     od_ssm_norm[j], batch, seq)
            b = _pool(u, od_pool_w[j], od_pool_b[j], od_pool_scale[j], seq)
            w_out = od_w_out[j]
        h = _outproj_ffn(a, b, h, w_out, ln1_g[layer], ln1_b[layer], ffn_w_up[layer], ffn_conv_w[layer],
                         ffn_conv_b[layer], ffn_w_down[layer], ln2_g[layer], ln2_b[layer], seq)
    return h.reshape(batch, seq, D_MODEL)
```

```python
import functools
import math

import jax
import jax.numpy as jnp
import numpy as np
from jax import lax
from jax.experimental import pallas as pl
from jax.experimental.pallas import tpu as pltpu

F32 = jnp.float32
BF16 = jnp.bfloat16

D_MODEL = 1024
DEPTH = 4
HALF = D_MODEL // 2
ML_HEADS = 4
ML_HD = HALF // ML_HEADS
FOX_HEADS = 8
FOX_HD = HALF // FOX_HEADS
SSM_HEADDIM = 64
SSM_HEADS = HALF // SSM_HEADDIM
SSM_GROUPS = 2
SSM_STATE = 128
SSM_CONV = 4
SSM_CONV_CH = HALF + 2 * SSM_GROUPS * SSM_STATE
POOL_WINDOWS = (2, 4, 8, 16)
POOL_GW = HALF // len(POOL_WINDOWS)
D_FF = 256 * ((8 * D_MODEL // 3 + 255) // 256)
FFN_CONV = 3
LN_EPS = 1e-5
ALPHA = (2.0 * DEPTH) ** 0.25

LANES = 128
SUBLANES = 8
NEG = -1e30
LOG2E = 1.4426950408889634
VMEM_LIMIT = 56 * 1024 * 1024

ROW_TILE = 1024
ML_CHUNK = 512
SSD_CHUNK = 256
FOX_BLOCK = 256
FOX_QUERIES = 512
FOX_UNROLL = 16
FFN_COLS = 256
EPILOGUE_SPLIT = 2


def _params(*sem):
    return pltpu.CompilerParams(dimension_semantics=sem, vmem_limit_bytes=VMEM_LIMIT)


def _dot(a, b):
    return jnp.dot(a, b, preferred_element_type=F32)


def _dot_nt(a, b):
    return lax.dot_general(a, b, (((1,), (1,)), ((), ())), preferred_element_type=F32)


def _split3(x):
    h1 = x.astype(BF16)
    r1 = x - h1.astype(F32)
    h2 = r1.astype(BF16)
    h3 = (r1 - h2.astype(F32)).astype(BF16)
    return h1, h2, h3


def _dot_exact_rhs(t, x):
    h1, h2, h3 = _split3(x)
    return _dot(t, h1) + _dot(t, h2) + _dot(t, h3)


def _dot_exact_lhs(x, t):
    h1, h2, h3 = _split3(x)
    return _dot(h1, t) + _dot(h2, t) + _dot(h3, t)


def _log_sigmoid(x):
    return jnp.minimum(x, 0.0) - jnp.log(1.0 + jnp.exp(-jnp.abs(x)))


def _softplus(x):
    return jnp.maximum(x, 0.0) + jnp.log(1.0 + jnp.exp(-jnp.abs(x)))


def _sigmoid(x):
    return 1.0 / (1.0 + jnp.exp(-x))


def _silu(x):
    return x * _sigmoid(x)


def _tri(n, lower):
    r = lax.broadcasted_iota(jnp.int32, (n, n), 0)
    c = lax.broadcasted_iota(jnp.int32, (n, n), 1)
    return (c <= r) if lower else (r <= c)


def _layer_norm(v, g, b):
    mu = jnp.mean(v, axis=-1, keepdims=True)
    d = v - mu
    var = jnp.mean(d * d, axis=-1, keepdims=True)
    return d * lax.rsqrt(var + LN_EPS) * g + b


EV_MAIN = 4 * HALF
EV_GATES = 2 * ML_HEADS + FOX_HEADS
FOX_QSCALE = FOX_HD ** -0.5 * LOG2E


def _inproj_even_kernel(x_ref, wm_ref, bm_ref, wo_ref, bo_ref, wg_ref, bg_ref, wgt_ref, bgt_ref,
                        wqvt_ref, bqvt_ref, qkv_ref, mo_ref, g_ref, gt_ref, qvt_ref):
    xb = x_ref[...].astype(BF16)
    for j in range(EV_MAIN // HALF):
        sl = slice(j * HALF, (j + 1) * HALF)
        qkv_ref[:, sl] = (_dot(xb, wm_ref[:, sl]) + bm_ref[:, sl]).astype(BF16)
    mo_ref[...] = _dot(xb, wo_ref[...]) + bo_ref[...]
    g_ref[...] = _dot(xb, wg_ref[...]) + bg_ref[...]
    gt_ref[...] = _dot_nt(wgt_ref[...], xb) + bgt_ref[...]
    fqt = (_dot_nt(wqvt_ref[0:HALF, :], xb) + bqvt_ref[0:HALF, :]) * FOX_QSCALE
    qvt_ref[0:HALF, :] = fqt.astype(BF16)
    qvt_ref[HALF:, :] = (_dot_nt(wqvt_ref[HALF:, :], xb) + bqvt_ref[HALF:, :]).astype(BF16)


def _inproj_even(x, w_in, b_in):
    m = x.shape[0]
    tm = ROW_TILE
    o = [0, HALF, 2 * HALF, 3 * HALF, 4 * HALF, 4 * HALF + ML_HEADS, 4 * HALF + 2 * ML_HEADS]
    fq0 = o[6]
    fk0 = fq0 + HALF
    fv0 = fq0 + 2 * HALF
    ff0 = fq0 + 3 * HALF
    wm = jnp.concatenate([w_in[:, :3 * HALF], w_in[:, fk0:fv0]], axis=1).astype(BF16)
    bm = jnp.concatenate([b_in[:3 * HALF], b_in[fk0:fv0]])[None, :]
    wvt = jnp.concatenate([w_in[:, fq0:fk0], w_in[:, fv0:ff0]], axis=1).T.astype(BF16)
    bvt = jnp.concatenate([b_in[fq0:fk0], b_in[fv0:ff0]])[:, None]
    wo = w_in[:, 3 * HALF:4 * HALF].astype(BF16)
    bo = b_in[3 * HALF:4 * HALF][None, :]
    wg16 = jnp.concatenate([w_in[:, o[4]:o[6]], w_in[:, ff0:]], axis=1)
    bg16 = jnp.concatenate([b_in[o[4]:o[6]], b_in[ff0:]])
    wg = jnp.pad(wg16, ((0, 0), (0, LANES - EV_GATES))).astype(BF16)
    bg = jnp.pad(bg16, (0, LANES - EV_GATES))[None, :]
    wgt = wg16.T.astype(BF16)
    bgt = bg16[:, None]
    full = lambda shape: pl.BlockSpec(shape, lambda i: (0, 0))
    return pl.pallas_call(
        _inproj_even_kernel,
        grid=(m // tm,),
        in_specs=[pl.BlockSpec((tm, D_MODEL), lambda i: (i, 0)),
                  full((D_MODEL, EV_MAIN)), full((1, EV_MAIN)),
                  full((D_MODEL, HALF)), full((1, HALF)),
                  full((D_MODEL, LANES)), full((1, LANES)),
                  full((EV_GATES, D_MODEL)), full((EV_GATES, 1)),
                  full((2 * HALF, D_MODEL)), full((2 * HALF, 1))],
        out_specs=[pl.BlockSpec((tm, EV_MAIN), lambda i: (i, 0)),
                   pl.BlockSpec((tm, HALF), lambda i: (i, 0)),
                   pl.BlockSpec((tm, LANES), lambda i: (i, 0)),
                   pl.BlockSpec((EV_GATES, tm), lambda i: (0, i)),
                   pl.BlockSpec((2 * HALF, tm), lambda i: (0, i))],
        out_shape=[jax.ShapeDtypeStruct((m, EV_MAIN), BF16),
                   jax.ShapeDtypeStruct((m, HALF), F32),
                   jax.ShapeDtypeStruct((m, LANES), F32),
                   jax.ShapeDtypeStruct((EV_GATES, m), F32),
                   jax.ShapeDtypeStruct((2 * HALF, m), BF16)],
        compiler_params=_params("arbitrary"),
        name="inproj_even",
    )(x, wm, bm, wo, bo, wg, bg, wgt, bgt, wvt, bvt)


ML_LOG_SCALE = math.log(ML_HD ** -0.5)


FOX_PAIRS = FOX_HEADS // 2
KEXT_W = 2 * LANES
BIAS_PARTS = 3


def _bias_routing():
    r = np.zeros((BIAS_PARTS, LANES, FOX_PAIRS * LANES), np.float32)
    for head in range(FOX_HEADS):
        for part in range(BIAS_PARTS):
            r[part, 2 * ML_HEADS + head, (head // 2) * LANES + BIAS_PARTS * (head % 2) + part] = 1.0
    return jnp.asarray(r, BF16)


def _mlstm_kernel(*refs, lc, nb):
    q_ref, k_ref, v_ref, fk_ref, mo_ref, g_ref = refs[:6]
    gt_refs = refs[6:6 + nb]
    gain_ref, route_ref, h_ref, kext_ref, ct_ref, m_ref, fcar_ref = refs[6 + nb:]
    c = pl.program_id(0)

    @pl.when(c == 0)
    def _():
        ct_ref[...] = jnp.zeros_like(ct_ref)
        m_ref[...] = jnp.zeros_like(m_ref)
        fcar_ref[...] = jnp.zeros_like(fcar_ref)

    tril = _tri(lc, True)
    tril_b = jnp.where(tril, 1.0, 0.0).astype(BF16)
    triu_b = jnp.where(_tri(lc, False), 1.0, 0.0).astype(BF16)
    batches = range(nb)
    g = [g_ref[b] for b in batches]
    gt = [gt_refs[b][...] for b in batches]
    bc_col = [_dot_exact_rhs(tril_b, _log_sigmoid(g[b])) for b in batches]
    bc_row = [_dot_exact_lhs(_log_sigmoid(gt[b]), triu_b) for b in batches]

    for b in batches:
        f_blk = bc_col[b] + fcar_ref[b, 0:1, :]
        fcar_ref[b] = jnp.broadcast_to(f_blk[lc - 1:lc, :], fcar_ref.shape[1:])
        parts = _split3(f_blk * LOG2E)
        bias = sum(_dot(parts[i], route_ref[i]) for i in range(BIAS_PARTS)).astype(BF16)
        for p in range(FOX_PAIRS):
            kext_ref[b, :, p * KEXT_W:p * KEXT_W + LANES] = fk_ref[b, :, p * LANES:(p + 1) * LANES]
            kext_ref[b, :, p * KEXT_W + LANES:(p + 1) * KEXT_W] = bias[:, p * LANES:(p + 1) * LANES]

    lane = lax.broadcasted_iota(jnp.int32, (lc, ML_HD), 1)
    ones_col = jnp.where(lane == 0, 1.0, 0.0).astype(BF16)
    streams = [(b, h) for b in batches for h in range(ML_HEADS)]
    ids = range(len(streams))
    hs = [slice(h * ML_HD, (h + 1) * ML_HD) for _, h in streams]
    vext = [jnp.concatenate([v_ref[b, :, hs[i]], ones_col], axis=1) for i, (b, _) in enumerate(streams)]
    bcl = [bc_col[b][:, ML_HEADS + h:ML_HEADS + h + 1] for b, h in streams]
    bcr = [bc_row[b][ML_HEADS + h:ML_HEADS + h + 1, :] for b, h in streams]
    iil = [g[b][:, h:h + 1] for b, h in streams]
    iir = [gt[b][h:h + 1, :] for b, h in streams]
    m_st = [m_ref[b, h:h + 1, 0:1] for b, h in streams]
    ct = [ct_ref[i] for i in ids]

    s = [_dot_nt(q_ref[b, :, hs[i]], k_ref[b, :, hs[i]]) for i, (b, _) in enumerate(streams)]
    inter = [_dot(q_ref[b, :, hs[i]], ct[i].astype(BF16)) for i, (b, _) in enumerate(streams)]

    for i, (b, h) in enumerate(streams):
        btot = bcr[i][:, lc - 1:lc]
        m_new = jnp.maximum(btot + m_st[i], jnp.max(btot - bcr[i] + iir[i], axis=1, keepdims=True))
        decay = jnp.exp(btot + m_st[i] - m_new)
        ws = jnp.exp(btot - bcl[i] + iil[i] - m_new)
        kwt = (k_ref[b, :, hs[i]].astype(F32) * ws).T.astype(BF16)
        ct_ref[i] = decay * ct[i] + _dot(kwt, vext[i])
        m_ref[b, h:h + 1, :] = jnp.broadcast_to(m_new, (1, LANES))

    mt, w_inter, intra = [], [], []
    for i in ids:
        a_col = bcl[i] + m_st[i]
        dmat = jnp.where(tril, bcl[i] + (iir[i] - bcr[i]), NEG)
        mt.append(jnp.maximum(a_col, jnp.max(dmat, axis=1, keepdims=True)))
        mts = mt[i] - ML_LOG_SCALE
        w_inter.append(jnp.exp(a_col - mts))
        pw = (s[i] * jnp.exp(dmat - mts)).astype(BF16)
        intra.append(_dot(pw, vext[i]))

    for i, (b, _) in enumerate(streams):
        tot = intra[i] + w_inter[i] * inter[i]
        num = tot[:, :ML_HD]
        den = tot[:, ML_HD:ML_HD + 1]
        hh = num / jnp.maximum(jnp.abs(den), jnp.exp(-mt[i]))
        mu = jnp.mean(hh, axis=-1, keepdims=True)
        d = hh - mu
        var = jnp.mean(d * d, axis=-1, keepdims=True)
        hn = d * lax.rsqrt(var + LN_EPS) * gain_ref[:, hs[i]]
        h_ref[b, :, hs[i]] = (_sigmoid(mo_ref[b, :, hs[i]]) * hn).astype(BF16)


def _mlstm(qkv, mo, g, gt, gain, batch, seq):
    lc = min(ML_CHUNK, seq)
    nc = seq // lc
    m = batch * seq
    row = lambda width, j: pl.BlockSpec((batch, lc, width), lambda c, j=j: (0, c, j))
    out = pl.pallas_call(
        functools.partial(_mlstm_kernel, lc=lc, nb=batch),
        grid=(nc,),
        in_specs=[row(HALF, 0), row(HALF, 1), row(HALF, 2), row(HALF, 3), row(HALF, 0), row(LANES, 0)]
        + [pl.BlockSpec((EV_GATES, lc), lambda c, b=b: (0, b * nc + c)) for b in range(batch)]
        + [pl.BlockSpec((1, HALF), lambda c: (0, 0)),
           pl.BlockSpec((BIAS_PARTS, LANES, FOX_PAIRS * LANES), lambda c: (0, 0, 0))],
        out_specs=[row(HALF, 0), row(FOX_PAIRS * KEXT_W, 0)],
        out_shape=[jax.ShapeDtypeStruct((batch, seq, HALF), BF16),
                   jax.ShapeDtypeStruct((batch, seq, FOX_PAIRS * KEXT_W), BF16)],
        scratch_shapes=[pltpu.VMEM((batch * ML_HEADS, ML_HD, 2 * ML_HD), F32),
                        pltpu.VMEM((batch, SUBLANES, LANES), F32),
                        pltpu.VMEM((batch, SUBLANES, LANES), F32)],
        compiler_params=_params("arbitrary"),
        name="mlstm",
    )(*([qkv.reshape(batch, seq, -1)] * 4), mo.reshape(batch, seq, HALF), g.reshape(batch, seq, LANES),
      *([gt] * batch), gain, _bias_routing())
    return out[0].reshape(m, HALF), out[1].reshape(m, FOX_PAIRS * KEXT_W)


BF16_ROWS = 2 * SUBLANES
FOX_ACC_ROWS = FOX_HD + BF16_ROWS


def _fox_kernel(qt_ref, kext_ref, vt_ref, o_ref, m_sc, acc_sc, s0_sc, s1_sc, mx0_sc, mx1_sc, *, blk, qw):
    ratio = qw // blk
    qi = pl.program_id(2)
    q2 = qt_ref[...]
    row = lax.broadcasted_iota(jnp.int32, (LANES, qw), 0)
    top = row < FOX_HD
    zero = jnp.zeros_like(q2)
    qe = []
    for hd in range(2):
        pick = (row >= BIAS_PARTS * hd) & (row < BIAS_PARTS * (hd + 1))
        minus1 = jnp.where(pick, -1.0, 0.0).astype(BF16)
        qh = jnp.where(top, q2, zero) if hd == 0 else jnp.where(top, zero, q2)
        qe.append(jnp.concatenate([qh, minus1], axis=0))

    m_sc[...] = jnp.full_like(m_sc, NEG)
    acc_sc[...] = jnp.zeros_like(acc_sc)
    s_bufs = (s0_sc, s1_sc)
    mx_bufs = (mx0_sc, mx1_sc)
    ones_rows = jnp.where(lax.broadcasted_iota(jnp.int32, (BF16_ROWS, blk), 0) == 0, 1.0, 0.0).astype(BF16)

    def scores(kj, slot, diag):
        k0 = pl.multiple_of(kj * blk, blk)
        ke = kext_ref[pl.ds(k0, blk), :]
        for hd in range(2):
            s = _dot(ke, qe[hd])
            if diag is not None:
                key = lax.broadcasted_iota(jnp.int32, (blk, qw), 0) + diag * blk
                s = jnp.where(key <= lax.broadcasted_iota(jnp.int32, (blk, qw), 1), s, NEG)
            s_bufs[slot][hd] = s
            mx_bufs[slot][hd] = jnp.max(s, axis=0, keepdims=True)

    def accumulate(kj, slot):
        k0 = pl.multiple_of(kj * blk, blk)
        for hd in range(2):
            rows = slice(hd * FOX_HD, (hd + 1) * FOX_HD)
            m_prev = m_sc[hd]
            m_next = jnp.maximum(m_prev, mx_bufs[slot][hd])
            alpha = jnp.exp2(m_prev - m_next)
            p = jnp.exp2(s_bufs[slot][hd] - m_next).astype(BF16)
            m_sc[hd] = m_next
            vt1 = jnp.concatenate([vt_ref[rows, pl.ds(k0, blk)], ones_rows], axis=0)
            acc_sc[hd] = alpha * acc_sc[hd] + _dot(vt1, p)

    n_off = ratio * qi

    def key_block(t):
        return jnp.where(t < ratio, ratio * qi + t, t - ratio)

    def step(t, slot, diag=None):
        scores(key_block(t), slot, diag)
        accumulate(key_block(t - 1), 1 - slot)

    scores(ratio * qi, 0, 0)
    for r in range(1, ratio):
        step(r, r % 2, r)

    def body(i, carry):
        for u in range(FOX_UNROLL):
            step(ratio + FOX_UNROLL * i + u, (ratio + u) % 2)
        return carry

    lax.fori_loop(0, n_off // FOX_UNROLL, body, 0)
    done = (n_off // FOX_UNROLL) * FOX_UNROLL
    width = FOX_UNROLL // 2
    while width >= 1:
        if width % math.gcd(ratio, FOX_UNROLL) == 0:
            @pl.when(n_off & width != 0)
            def _(done=done, width=width):
                for u in range(width):
                    step(ratio + done + u, (ratio + u) % 2)
            done = done + (n_off & width)
        width //= 2

    last = ratio + n_off - 1
    if ratio % 2 == 0:
        accumulate(key_block(last), (ratio - 1) % 2)
    else:
        for parity in range(2):
            @pl.when(last % 2 == parity)
            def _(parity=parity):
                accumulate(key_block(last), parity)

    out = [acc_sc[hd, 0:FOX_HD, :] * (1.0 / acc_sc[hd, FOX_HD:FOX_HD + 1, :]) for hd in range(2)]
    o_ref[...] = jnp.concatenate(out, axis=0).T.astype(BF16)


def _fox(qvt, kext, batch, seq):
    blk = min(FOX_BLOCK, seq)
    qw = min(FOX_QUERIES, seq)
    nq = seq // qw
    m = batch * seq
    return pl.pallas_call(
        functools.partial(_fox_kernel, blk=blk, qw=qw),
        grid=(batch, FOX_PAIRS, nq),
        in_specs=[pl.BlockSpec((LANES, qw), lambda b, p, i: (p, b * nq + i)),
                  pl.BlockSpec((seq, KEXT_W), lambda b, p, i: (b, p)),
                  pl.BlockSpec((LANES, seq), lambda b, p, i: (FOX_PAIRS + p, b))],
        out_specs=pl.BlockSpec((qw, LANES), lambda b, p, i: (b * nq + i, p)),
        out_shape=jax.ShapeDtypeStruct((m, HALF), BF16),
        scratch_shapes=[pltpu.VMEM((2, 1, qw), F32),
                        pltpu.VMEM((2, FOX_ACC_ROWS, qw), F32),
                        pltpu.VMEM((2, blk, qw), F32),
                        pltpu.VMEM((2, blk, qw), F32),
                        pltpu.VMEM((2, 1, qw), F32),
                        pltpu.VMEM((2, 1, qw), F32)],
        compiler_params=_params("arbitrary", "arbitrary", "arbitrary"),
        name="fox_attention",
    )(qvt, kext, qvt)


def _ffn_kernel(a_ref, b_ref, xin_ref, wo_ref, g1_ref, beta1_ref, wu_ref, cw_ref, cb_ref, wd_ref, g_ref, beta_ref,
                o_ref, he_ref, hprev_ref, act_ref, x_ref, *, tm, tiles_per_seq):
    i = pl.program_id(0)

    @pl.when(i % tiles_per_seq == 0)
    def _():
        hprev_ref[...] = jnp.zeros_like(hprev_ref)

    for r in range(EPILOGUE_SPLIT):
        rows = slice(r * tm // EPILOGUE_SPLIT, (r + 1) * tm // EPILOGUE_SPLIT)
        ab = jnp.concatenate([a_ref[rows, :], b_ref[rows, :]], axis=1)
        x_ref[rows, :] = _layer_norm(ALPHA * xin_ref[rows, :] + _dot(ab, wo_ref[...]), g1_ref[...], beta1_ref[...])

    xb = x_ref[...].astype(BF16)
    cw = FFN_COLS

    def conv_cols(cs):
        he_ref[0:SUBLANES, :] = hprev_ref[:, cs]
        he_ref[SUBLANES:, :] = _dot(xb, wu_ref[:, cs])
        hprev_ref[:, cs] = he_ref[tm:tm + SUBLANES, :]
        he = he_ref[...]
        conv = cb_ref[:, cs] + cw_ref[FFN_CONV - 1:FFN_CONV, cs] * he[SUBLANES:, :]
        for back in range(1, FFN_CONV):
            conv = conv + cw_ref[FFN_CONV - 1 - back:FFN_CONV - back, cs] * pltpu.roll(he, back, 0)[SUBLANES:, :]
        return conv

    for j in range(D_FF // cw):
        val = conv_cols(slice(j * cw, (j + 1) * cw))
        gate = conv_cols(slice(D_FF + j * cw, D_FF + (j + 1) * cw))
        act_ref[:, j * cw:(j + 1) * cw] = (_silu(gate) * val).astype(BF16)
    for r in range(EPILOGUE_SPLIT):
        rows = slice(r * tm // EPILOGUE_SPLIT, (r + 1) * tm // EPILOGUE_SPLIT)
        f = _dot(act_ref[rows, :], wd_ref[...])
        o_ref[rows, :] = _layer_norm(ALPHA * x_ref[rows, :] + f, g_ref[...], beta_ref[...])


def _outproj_ffn(a, b, x, w_out, ln1_g, ln1_b, w_up, conv_w, conv_b, w_down, ln2_g, ln2_b, seq):
    m = x.shape[0]
    tm = min(ROW_TILE, seq)
    row = lambda width: pl.BlockSpec((tm, width), lambda i: (i, 0))
    full = lambda shape: pl.BlockSpec(shape, lambda i: (0, 0), pipeline_mode=pl.Buffered(1))
    return pl.pallas_call(
        functools.partial(_ffn_kernel, tm=tm, tiles_per_seq=seq // tm),
        grid=(m // tm,),
        in_specs=[row(HALF), row(HALF), row(D_MODEL),
                  full((D_MODEL, D_MODEL)), full((1, D_MODEL)), full((1, D_MODEL)),
                  full((D_MODEL, 2 * D_FF)), full((FFN_CONV, 2 * D_FF)), full((1, 2 * D_FF)),
                  full((D_FF, D_MODEL)), full((1, D_MODEL)), full((1, D_MODEL))],
        out_specs=row(D_MODEL),
        out_shape=jax.ShapeDtypeStruct((m, D_MODEL), F32),
        scratch_shapes=[pltpu.VMEM((tm + SUBLANES, FFN_COLS), F32),
                        pltpu.VMEM((SUBLANES, 2 * D_FF), F32),
                        pltpu.VMEM((tm, D_FF), BF16),
                        pltpu.VMEM((tm, D_MODEL), F32)],
        compiler_params=_params("arbitrary"),
        name="outproj_conv_ffn",
    )(a, b, x, w_out.astype(BF16), ln1_g[None, :], ln1_b[None, :],
      w_up.astype(BF16), conv_w, conv_b[None, :], w_down.astype(BF16), ln2_g[None, :], ln2_b[None, :])


OD_MAIN = SSM_CONV_CH + 2 * HALF


def _inproj_odd_kernel(x_ref, wm_ref, wd_ref, wdt_ref, u_ref, dt_ref, dtt_ref):
    xb = x_ref[...].astype(BF16)
    for j in range(OD_MAIN // HALF):
        sl = slice(j * HALF, (j + 1) * HALF)
        u_ref[:, sl] = _dot(xb, wm_ref[:, sl])
    dt_ref[...] = _dot(xb, wd_ref[...])
    dtt_ref[...] = _dot_nt(wdt_ref[...], xb)


def _inproj_odd(x, w_in):
    m = x.shape[0]
    tm = ROW_TILE
    z1 = HALF
    x1 = z1 + SSM_CONV_CH
    d1 = x1 + SSM_HEADS
    wm = jnp.concatenate([w_in[:, z1:x1], w_in[:, :z1], w_in[:, d1:]], axis=1).astype(BF16)
    wd8 = w_in[:, x1:d1]
    wd = jnp.pad(wd8, ((0, 0), (0, LANES - SSM_HEADS))).astype(BF16)
    wdt = wd8.T.astype(BF16)
    full = lambda shape: pl.BlockSpec(shape, lambda i: (0, 0))
    return pl.pallas_call(
        _inproj_odd_kernel,
        grid=(m // tm,),
        in_specs=[pl.BlockSpec((tm, D_MODEL), lambda i: (i, 0)),
                  full((D_MODEL, OD_MAIN)), full((D_MODEL, LANES)), full((SSM_HEADS, D_MODEL))],
        out_specs=[pl.BlockSpec((tm, OD_MAIN), lambda i: (i, 0)),
                   pl.BlockSpec((tm, LANES), lambda i: (i, 0)),
                   pl.BlockSpec((SSM_HEADS, tm), lambda i: (0, i))],
        out_shape=[jax.ShapeDtypeStruct((m, OD_MAIN), F32),
                   jax.ShapeDtypeStruct((m, LANES), F32),
                   jax.ShapeDtypeStruct((SSM_HEADS, m), F32)],
        compiler_params=_params("arbitrary"),
        name="inproj_odd",
    )(x, wm, wd, wdt)


SSM_HPG = SSM_HEADS // SSM_GROUPS
GROUP_W = SSM_HPG * SSM_HEADDIM


def _ssd_kernel(xbc_ref, z_ref, dt_ref, dtt_ref, cw_ref, cb_ref, dtb_ref, dtbt_ref, a_ref, at_ref,
                dskip_ref, gain_ref, y_ref, xe_ref, ht_ref, *, lc):
    c = pl.program_id(1)

    @pl.when(c == 0)
    def _():
        xe_ref[0:SUBLANES, :] = jnp.zeros((SUBLANES, SSM_CONV_CH), F32)
        ht_ref[...] = jnp.zeros_like(ht_ref)

    xe_ref[SUBLANES:, :] = xbc_ref[...]
    xe = xe_ref[...]
    conv = cb_ref[...] + cw_ref[SSM_CONV - 1:SSM_CONV, :] * xe[SUBLANES:, :]
    for back in range(1, SSM_CONV):
        conv = conv + cw_ref[SSM_CONV - 1 - back:SSM_CONV - back, :] * pltpu.roll(xe, back, 0)[SUBLANES:, :]
    xe_ref[0:SUBLANES, :] = xe_ref[lc:lc + SUBLANES, :]
    xact = _silu(conv)
    xs = xact[:, :HALF]

    tril = _tri(lc, True)
    tril_b = jnp.where(tril, 1.0, 0.0).astype(BF16)
    triu_b = jnp.where(_tri(lc, False), 1.0, 0.0).astype(BF16)
    er = lax.broadcasted_iota(jnp.int32, (LANES, HALF), 0)
    ec = lax.broadcasted_iota(jnp.int32, (LANES, HALF), 1)
    head_of = jnp.right_shift(ec, SSM_HEADDIM.bit_length() - 1)
    expand = jnp.where(head_of == er, 1.0, 0.0).astype(BF16)

    dt_col = _softplus(dt_ref[...] + dtb_ref[...])
    dt_row = _softplus(dtt_ref[...] + dtbt_ref[...])
    acs_col = _dot_exact_rhs(tril_b, dt_col * a_ref[...])
    acs_row = _dot_exact_lhs(dt_row * at_ref[...], triu_b)
    acs_x = _dot_exact_lhs(acs_col, expand)
    dt_x = _dot_exact_lhs(dt_col, expand)
    xdt = xs * dt_x
    left = lax.broadcasted_iota(jnp.int32, (lc, LANES), 1) < SSM_HEADDIM

    groups = range(SSM_GROUPS)
    gs = [slice(gi * GROUP_W, (gi + 1) * GROUP_W) for gi in groups]
    bm = [xact[:, HALF + gi * SSM_STATE:HALF + (gi + 1) * SSM_STATE] for gi in groups]
    cmb = [xact[:, HALF + (SSM_GROUPS + gi) * SSM_STATE:HALF + (SSM_GROUPS + gi + 1) * SSM_STATE].astype(BF16)
           for gi in groups]
    ht = [ht_ref[gi] for gi in groups]
    cbm = [_dot_nt(cmb[gi], bm[gi].astype(BF16)) for gi in groups]
    off = [_dot(cmb[gi], ht[gi].astype(BF16)) for gi in groups]
    for gi in groups:
        acs_g = acs_x[:, gs[gi]]
        last = acs_g[lc - 1:lc, :]
        xw = (xdt[:, gs[gi]] * jnp.exp(last - acs_g)).astype(BF16)
        ht_ref[gi] = jnp.exp(last) * ht[gi] + _dot(bm[gi].T.astype(BF16), xw)
    ys = []
    for gi in groups:
        pair_out = []
        for pj in range(SSM_HPG // 2):
            lo = gi * GROUP_W + pj * LANES
            xpair = xdt[:, lo:lo + LANES].astype(BF16)
            yh = []
            for e in (gi * SSM_HPG + 2 * pj, gi * SSM_HPG + 2 * pj + 1):
                seg = jnp.where(tril, acs_col[:, e:e + 1] - acs_row[e:e + 1, :], NEG)
                yh.append(_dot((cbm[gi] * jnp.exp(seg)).astype(BF16), xpair))
            pair_out.append(jnp.where(left, yh[0], yh[1]))
        y_diag = jnp.concatenate(pair_out, axis=1)
        ys.append(y_diag + off[gi] * jnp.exp(acs_x[:, gs[gi]]))
    y = jnp.concatenate(ys, axis=1) + dskip_ref[...] * xs
    gated = y * _silu(z_ref[...])
    ms = jnp.mean(gated * gated, axis=-1, keepdims=True)
    y_ref[...] = (gated * lax.rsqrt(ms + LN_EPS) * gain_ref[...]).astype(BF16)


def _ssd(u, dt, dtt, conv_w, conv_b, dt_bias, a_log, d_skip, ssm_norm, batch, seq):
    lc = min(SSD_CHUNK, seq)
    nc = seq // lc
    m = batch * seq
    a = -jnp.exp(a_log.astype(F32))
    pad = lambda v: jnp.pad(v, (0, LANES - SSM_HEADS))[None, :]
    full = lambda shape: pl.BlockSpec(shape, lambda b, c: (0, 0))
    return pl.pallas_call(
        functools.partial(_ssd_kernel, lc=lc),
        grid=(batch, nc),
        in_specs=[pl.BlockSpec((lc, SSM_CONV_CH), lambda b, c: (b * nc + c, 0)),
                  pl.BlockSpec((lc, HALF), lambda b, c: (b * nc + c, SSM_CONV_CH // HALF)),
                  pl.BlockSpec((lc, LANES), lambda b, c: (b * nc + c, 0)),
                  pl.BlockSpec((SSM_HEADS, lc), lambda b, c: (0, b * nc + c)),
                  full((SSM_CONV, SSM_CONV_CH)), full((1, SSM_CONV_CH)),
                  full((1, LANES)), full((SSM_HEADS, 1)), full((1, LANES)), full((SSM_HEADS, 1)),
                  full((1, HALF)), full((1, HALF))],
        out_specs=pl.BlockSpec((lc, HALF), lambda b, c: (b * nc + c, 0)),
        out_shape=jax.ShapeDtypeStruct((m, HALF), BF16),
        scratch_shapes=[pltpu.VMEM((lc + SUBLANES, SSM_CONV_CH), F32),
                        pltpu.VMEM((SSM_GROUPS, SSM_STATE, GROUP_W), F32)],
        compiler_params=_params("arbitrary", "arbitrary"),
        name="ssd",
    )(u, u, dt, dtt, conv_w, conv_b[None, :], pad(dt_bias), dt_bias[:, None], pad(a), a[:, None],
      jnp.repeat(d_skip, SSM_HEADDIM)[None, :], ssm_norm[None, :])


POOL_HALO = 16


def _pool_kernel(u_ref, w_ref, b_ref, sc_ref, p_ref, ue_ref, *, tm, tiles_per_seq):
    i = pl.program_id(0)

    @pl.when(i % tiles_per_seq == 0)
    def _():
        ue_ref[0:POOL_HALO, :] = jnp.zeros((POOL_HALO, HALF), F32)

    ue_ref[POOL_HALO:, :] = u_ref[...]
    t = (i % tiles_per_seq) * tm + lax.broadcasted_iota(jnp.int32, (tm, 1), 0)
    for gi, win in enumerate(POOL_WINDOWS):
        gs = slice(gi * POOL_GW, (gi + 1) * POOL_GW)
        acc = ue_ref[:, gs]
        span = 1
        while span < win:
            acc = acc + pltpu.roll(acc, span, 0)
            span *= 2
        cnt = jnp.minimum(t + 1, win).astype(F32)
        pooled = acc[POOL_HALO:, :] / cnt - ue_ref[POOL_HALO:POOL_HALO + tm, gs]
        y = _dot(pooled.astype(BF16), w_ref[gi]) + b_ref[:, gs]
        p_ref[:, gs] = (y * sc_ref[:, gs]).astype(BF16)
    ue_ref[0:POOL_HALO, :] = ue_ref[tm:tm + POOL_HALO, :]


def _pool(u, pool_w, pool_b, pool_scale, seq):
    m = u.shape[0]
    tm = min(ROW_TILE, seq)
    full2 = lambda shape: pl.BlockSpec(shape, lambda i: (0, 0))
    return pl.pallas_call(
        functools.partial(_pool_kernel, tm=tm, tiles_per_seq=seq // tm),
        grid=(m // tm,),
        in_specs=[pl.BlockSpec((tm, HALF), lambda i: (i, (SSM_CONV_CH + HALF) // HALF)),
                  pl.BlockSpec((len(POOL_WINDOWS), POOL_GW, POOL_GW), lambda i: (0, 0, 0)),
                  full2((1, HALF)), full2((1, HALF))],
        out_specs=pl.BlockSpec((tm, HALF), lambda i: (i, 0)),
        out_shape=jax.ShapeDtypeStruct((m, HALF), BF16),
        scratch_shapes=[pltpu.VMEM((tm + POOL_HALO, HALF), F32)],
        compiler_params=_params("arbitrary"),
        name="pool",
    )(u, pool_w.astype(BF16), pool_b[None, :], pool_scale[None, :])


def kernel(x, ev_w_in, ev_b_in, ev_ml_norm, ev_w_out, od_w_in, od_conv_w, od_conv_b, od_dt_bias, od_a_log,
           od_d_skip, od_ssm_norm, od_pool_w, od_pool_b, od_pool_scale, od_w_out, ffn_w_up, ffn_conv_w,
           ffn_conv_b, ffn_w_down, ln1_g, ln1_b, ln2_g, ln2_b):
    batch, seq, _ = x.shape
    h = x.reshape(batch * seq, D_MODEL)
    for layer in range(DEPTH):
        j = layer // 2
        if layer % 2 == 0:
            qkv, mo, g, gt, qvt = _inproj_even(h, ev_w_in[j], ev_b_in[j])
            a, kext = _mlstm(qkv, mo, g, gt, ev_ml_norm[j][None, :], batch, seq)
            b = _fox(qvt, kext, batch, seq)
            w_out = ev_w_out[j]
        else:
            u, dt, dtt = _inproj_odd(h, od_w_in[j])
            a = _ssd(u, dt, dtt, od_conv_w[j], od_conv_b[j], od_dt_bias[j], od_a_log[j], od_d_skip[j],
                     od_ssm_norm[j], batch, seq)
            b = _pool(u, od_pool_w[j], od_pool_b[j], od_pool_scale[j], seq)
            w_out = od_w_out[j]
        h = _outproj_ffn(a, b, h, w_out, ln1_g[layer], ln1_b[layer], ffn_w_up[layer], ffn_conv_w[layer],
                         ffn_conv_b[layer], ffn_w_down[layer], ln2_g[layer], ln2_b[layer], seq)
    return h.reshape(batch, seq, D_MODEL)
```
